```python
import jax
import jax.numpy as jnp
from jax import lax
import numpy as np

D_MODEL = 2048
BATCH = 4
SEQ = 8192
DEPTH = 4

N_MIXERS = 3
N_HEADS = 16
HEAD_DIM = D_MODEL // N_HEADS
D_FF = 4 * D_MODEL
CONV_WIDTH = 31
MOBA_BLOCK = 256
MOBA_TOPK = 3
MOBA_Q_CHUNK = 16
FOX_Q_BLOCK = 128
NORM_EPS = 1e-6
NEG_INF = -1e30
LAYER_MIXERS = tuple(i % N_MIXERS for i in range(DEPTH))
N_CONV = LAYER_MIXERS.count(0)
N_MOBA = LAYER_MIXERS.count(1)
N_FOX = LAYER_MIXERS.count(2)

kernel_name = 'hybrid_conv_moba_fox_decoder'


def _rms_norm(x, g):
    xf = x.astype(jnp.float32)
    y = xf * lax.rsqrt(jnp.mean(xf * xf, axis=-1, keepdims=True) + NORM_EPS)
    return (y * g.astype(jnp.float32)).astype(x.dtype)


def _layer_norm(x, g, b):
    xf = x.astype(jnp.float32)
    mu = jnp.mean(xf, axis=-1, keepdims=True)
    var = jnp.mean(jnp.square(xf - mu), axis=-1, keepdims=True)
    y = (xf - mu) * lax.rsqrt(var + NORM_EPS)
    return (y * g.astype(jnp.float32) + b.astype(jnp.float32)).astype(x.dtype)


def _split_heads(z):
    b, s, _ = z.shape
    return z.reshape(b, s, N_HEADS, HEAD_DIM).transpose(0, 2, 1, 3)


def _merge_blocks(o):
    nb, b, h, q, d = o.shape
    return o.transpose(1, 0, 3, 2, 4).reshape(b, nb * q, h * d)


def _conv_mixer(h, w_in, b_in, dw_w, dw_b, ln_g, ln_b, w_out, b_out):
    u = h @ w_in + b_in
    a, gt = jnp.split(u, 2, axis=-1)
    u = a * jax.nn.sigmoid(gt)
    u = lax.conv_general_dilated(
        u, dw_w[:, None, :].astype(u.dtype), window_strides=(1,),
        padding=[(CONV_WIDTH - 1, 0)],
        dimension_numbers=('NWC', 'WIO', 'NWC'),
        feature_group_count=D_MODEL) + dw_b
    u = jax.nn.silu(_layer_norm(u, ln_g, ln_b))
    return u @ w_out + b_out


def _moba_mixer(h, w_qkv, q_g, k_g, w_o):
    b, s, _ = h.shape
    q, k, v = jnp.split(h @ w_qkv, 3, axis=-1)
    q = _rms_norm(_split_heads(q), q_g)
    k = _rms_norm(_split_heads(k), k_g)
    v = _split_heads(v)
    nb = -(-s // MOBA_BLOCK)
    pad = nb * MOBA_BLOCK - s
    kp = jnp.pad(k, ((0, 0), (0, 0), (0, pad), (0, 0)))
    vp = jnp.pad(v, ((0, 0), (0, 0), (0, pad), (0, 0)))
    kb = kp.reshape(b, N_HEADS, nb, MOBA_BLOCK, HEAD_DIM)
    vb = vp.reshape(b, N_HEADS, nb, MOBA_BLOCK, HEAD_DIM)
    kmean = jnp.mean(kb, axis=3)
    topk = min(MOBA_TOPK, nb)
    scale = HEAD_DIM ** -0.5
    n_chunks = s // MOBA_Q_CHUNK
    qc = q.reshape(b, N_HEADS, n_chunks, MOBA_Q_CHUNK, HEAD_DIM).transpose(2, 0, 1, 3, 4)
    blk_ids = jnp.arange(nb)
    gather = jax.vmap(jax.vmap(lambda kk, ii: kk[ii]))

    def chunk(args):
        ci, qi = args
        t0 = ci * MOBA_Q_CHUNK
        cur = t0 // MOBA_BLOCK
        gate = jnp.einsum('bhqd,bhnd->bhqn', qi, kmean).astype(jnp.float32)
        gate = jnp.where(blk_ids < cur, gate, NEG_INF)
        _, sel = lax.top_k(gate, topk)
        valid = sel < cur
        k_sel = gather(kb, sel)
        v_sel = gather(vb, sel)
        s_sel = jnp.einsum('bhqd,bhqkjd->bhqkj', qi, k_sel).astype(jnp.float32) * scale
        s_sel = jnp.where(valid[..., None], s_sel, NEG_INF)
        s_sel = s_sel.reshape(b, N_HEADS, MOBA_Q_CHUNK, topk * MOBA_BLOCK)
        k_own = lax.dynamic_slice_in_dim(kp, cur * MOBA_BLOCK, MOBA_BLOCK, axis=2)
        v_own = lax.dynamic_slice_in_dim(vp, cur * MOBA_BLOCK, MOBA_BLOCK, axis=2)
        s_own = jnp.einsum('bhqd,bhjd->bhqj', qi, k_own).astype(jnp.float32) * scale
        t_pos = t0 + jnp.arange(MOBA_Q_CHUNK)
        s_pos = cur * MOBA_BLOCK + jnp.arange(MOBA_BLOCK)
        s_own = jnp.where(s_pos[None, :] <= t_pos[:, None], s_own, NEG_INF)
        p = jax.nn.softmax(jnp.concatenate([s_sel, s_own], axis=-1), axis=-1)
        p_sel = p[..., :topk * MOBA_BLOCK].reshape(b, N_HEADS, MOBA_Q_CHUNK, topk, MOBA_BLOCK)
        p_own = p[..., topk * MOBA_BLOCK:]
        o = jnp.einsum('bhqkj,bhqkjd->bhqd', p_sel.astype(v.dtype), v_sel)
        o = o + jnp.einsum('bhqj,bhjd->bhqd', p_own.astype(v.dtype), v_own)
        return o

    o = lax.map(chunk, (jnp.arange(n_chunks), qc))
    return _merge_blocks(o) @ w_o


def _fox_mixer(h, w_in, b_f, q_g, k_g, w_o):
    b, s, _ = h.shape
    proj = h @ w_in
    d = D_MODEL
    q = _rms_norm(_split_heads(proj[..., :d]), q_g)
    k = _rms_norm(_split_heads(proj[..., d:2 * d]), k_g)
    v = _split_heads(proj[..., 2 * d:3 * d])
    g_out = proj[..., 3 * d:4 * d]
    log_f = jax.nn.log_sigmoid((proj[..., 4 * d:] + b_f).astype(jnp.float32))
    cum = jnp.cumsum(log_f, axis=1).transpose(0, 2, 1)
    scale = HEAD_DIM ** -0.5
    nqb = s // FOX_Q_BLOCK
    qb = q.reshape(b, N_HEADS, nqb, FOX_Q_BLOCK, HEAD_DIM).transpose(2, 0, 1, 3, 4)
    cq = cum.reshape(b, N_HEADS, nqb, FOX_Q_BLOCK).transpose(2, 0, 1, 3)
    s_pos = jnp.arange(s)

    def block(args):
        bi, qi, ci = args
        t_pos = bi * FOX_Q_BLOCK + jnp.arange(FOX_Q_BLOCK)
        logits = jnp.einsum('bhqd,bhsd->bhqs', qi, k).astype(jnp.float32) * scale
        logits = logits + (ci[..., :, None] - cum[:, :, None, :])
        logits = jnp.where(s_pos[None, :] <= t_pos[:, None], logits, NEG_INF)
        p = jax.nn.softmax(logits, axis=-1)
        return jnp.einsum('bhqs,bhsd->bhqd', p.astype(v.dtype), v)

    o = _merge_blocks(lax.map(block, (jnp.arange(nqb), qb, cq)))
    o = o * jax.nn.sigmoid(g_out)
    return o @ w_o


def _normal(k, shape, scale):
    return scale * jax.random.normal(k, shape, jnp.float32)


def setup_inputs(seed: int = 0) -> dict:
    key = jax.random.key(seed)
    ks = iter(jax.random.split(key, 32))
    d = D_MODEL
    sd = d ** -0.5
    return {
        'x': _normal(next(ks), (BATCH, SEQ, d), 1.0),
        'c': _normal(next(ks), (BATCH, d), 1.0),
        'norm1_g': 1.0 + _normal(next(ks), (DEPTH, d), 0.02),
        'norm2_g': 1.0 + _normal(next(ks), (DEPTH, d), 0.02),
        'mod_w': _normal(next(ks), (DEPTH, d, 6 * d), 0.5 * sd),
        'mod_b': _normal(next(ks), (DEPTH, 6 * d), 0.02),
        'mlp_w1': _normal(next(ks), (DEPTH, d, D_FF), sd),
        'mlp_w2': _normal(next(ks), (DEPTH, D_FF, d), D_FF ** -0.5),
        'conv_w_in': _normal(next(ks), (N_CONV, d, 2 * d), sd),
        'conv_b_in': _normal(next(ks), (N_CONV, 2 * d), 0.02),
        'conv_dw_w': _normal(next(ks), (N_CONV, CONV_WIDTH, d), CONV_WIDTH ** -0.5),
        'conv_dw_b': _normal(next(ks), (N_CONV, d), 0.02),
        'conv_ln_g': 1.0 + _normal(next(ks), (N_CONV, d), 0.02),
        'conv_ln_b': _normal(next(ks), (N_CONV, d), 0.02),
        'conv_w_out': _normal(next(ks), (N_CONV, d, d), sd),
        'conv_b_out': _normal(next(ks), (N_CONV, d), 0.02),
        'moba_w_qkv': _normal(next(ks), (N_MOBA, d, 3 * d), sd),
        'moba_q_g': 1.0 + _normal(next(ks), (N_MOBA, HEAD_DIM), 0.02),
        'moba_k_g': 1.0 + _normal(next(ks), (N_MOBA, HEAD_DIM), 0.02),
        'moba_w_o': _normal(next(ks), (N_MOBA, d, d), sd),
        'fox_w_in': _normal(next(ks), (N_FOX, d, 4 * d + N_HEADS), sd),
        'fox_b_f': 2.0 + _normal(next(ks), (N_FOX, N_HEADS), 0.5),
        'fox_q_g': 1.0 + _normal(next(ks), (N_FOX, HEAD_DIM), 0.02),
        'fox_k_g': 1.0 + _normal(next(ks), (N_FOX, HEAD_DIM), 0.02),
        'fox_w_o': _normal(next(ks), (N_FOX, d, d), sd),
    }


def reference(x, c, norm1_g, norm2_g, mod_w, mod_b, mlp_w1, mlp_w2,
              conv_w_in, conv_b_in, conv_dw_w, conv_dw_b, conv_ln_g, conv_ln_b,
              conv_w_out, conv_b_out, moba_w_qkv, moba_q_g, moba_k_g, moba_w_o,
              fox_w_in, fox_b_f, fox_q_g, fox_k_g, fox_w_o):
    cond = jax.nn.silu(c)
    for i in range(DEPTH):
        kind = LAYER_MIXERS[i]
        j = i // N_MIXERS
        mod = cond @ mod_w[i] + mod_b[i]
        sh1, sc1, g1, sh2, sc2, g2 = jnp.split(mod, 6, axis=-1)
        h = _rms_norm(x, norm1_g[i]) * (1.0 + sc1[:, None, :]) + sh1[:, None, :]
        if kind == 0:
            y = _conv_mixer(h, conv_w_in[j], conv_b_in[j], conv_dw_w[j], conv_dw_b[j],
                            conv_ln_g[j], conv_ln_b[j], conv_w_out[j], conv_b_out[j])
        elif kind == 1:
            y = _moba_mixer(h, moba_w_qkv[j], moba_q_g[j], moba_k_g[j], moba_w_o[j])
        else:
            y = _fox_mixer(h, fox_w_in[j], fox_b_f[j], fox_q_g[j], fox_k_g[j], fox_w_o[j])
        x = x + (1.0 + g1[:, None, :]) * y
        h = _rms_norm(x, norm2_g[i]) * (1.0 + sc2[:, None, :]) + sh2[:, None, :]
        u = jnp.square(jax.nn.relu(h @ mlp_w1[i]))
        x = x + (1.0 + g2[:, None, :]) * (u @ mlp_w2[i])
    return x
```

```python
import functools

import jax
import jax.numpy as jnp
from jax import lax
from jax.experimental import pallas as pl
from jax.experimental.pallas import tpu as pltpu

N_MIXERS = 3
MOBA_BLOCK = 256
MOBA_TOPK = 3
NORM_EPS = 1e-6
NEG_INF = -1e30

LANES = 128
SUBLANES = 8
V7X_VMEM_BYTES = 64 * 1024 * 1024
VMEM_LIMIT = 56 * 1024 * 1024

BF16 = jnp.bfloat16
F32 = jnp.float32


def _pick(n, pref, mult):
    if n <= pref:
        return n
    t = (pref // mult) * mult
    while t > mult and n % t:
        t -= mult
    assert n % t == 0, (n, pref, mult)
    return t


def _params(*sem):
    return pltpu.CompilerParams(dimension_semantics=sem, vmem_limit_bytes=VMEM_LIMIT)


def _dot(a, b):
    return jnp.dot(a, b, preferred_element_type=F32)


def _dot_t(a, b):
    return lax.dot_general(a, b, (((1,), (1,)), ((), ())), preferred_element_type=F32)


def _sigmoid(x):
    return 1.0 / (1.0 + jnp.exp(-x))


def _log_sigmoid(x):
    return -(jnp.maximum(-x, 0.0) + jnp.log1p(jnp.exp(-jnp.abs(x))))


def _normmod(x, ng, sc, sh):
    ms = jnp.mean(x * x, axis=-1, keepdims=True)
    y = x * lax.rsqrt(ms + NORM_EPS) * ng
    return y * (1.0 + sc) + sh


def _mod_kernel(c_ref, w_ref, b_ref, o_ref):
    c = c_ref[...]
    cond = c * _sigmoid(c)
    o_ref[0] = _dot(cond, w_ref[0]) + b_ref[0]


def _mod_vectors(c, mod_w, mod_b):
    depth, d, n = mod_w.shape
    b = c.shape[0]
    bp = -(-b // SUBLANES) * SUBLANES
    cp = jnp.pad(c, ((0, bp - b), (0, 0)))
    tn = _pick(n, 1024, LANES)
    out = pl.pallas_call(
        _mod_kernel,
        out_shape=jax.ShapeDtypeStruct((depth, bp, n), F32),
        grid=(depth, n // tn),
        in_specs=[
            pl.BlockSpec((bp, d), lambda i, j: (0, 0)),
            pl.BlockSpec((1, d, tn), lambda i, j: (i, 0, j)),
            pl.BlockSpec((1, 1, tn), lambda i, j: (i, 0, j)),
        ],
        out_specs=pl.BlockSpec((1, bp, tn), lambda i, j: (i, 0, j)),
        compiler_params=_params("parallel", "parallel"),
        name="mod_vectors",
    )(cp, mod_w, mod_b.reshape(depth, 1, n))
    return out[:, :b]


def _glu_kernel(x_ref, ng_ref, sc_ref, sh_ref, wa_ref, wg_ref, ba_ref, bg_ref, o_ref, h_scr):
    @pl.when(pl.program_id(1) == 0)
    def _():
        h_scr[...] = _normmod(x_ref[...], ng_ref[...], sc_ref[0], sh_ref[0]).astype(BF16)

    h = h_scr[...]
    a = _dot(h, wa_ref[...]) + ba_ref[...]
    g = _dot(h, wg_ref[...]) + bg_ref[...]
    o_ref[...] = a * _sigmoid(g)


def _conv_in(x, ng, modv, w_in, b_in, seq):
    m, d = x.shape
    tm = _pick(seq, 1024, SUBLANES)
    tn = _pick(d, 512, LANES)
    nj = d // tn
    tps = seq // tm
    b2 = b_in.reshape(1, 2 * d)
    return pl.pallas_call(
        _glu_kernel,
        out_shape=jax.ShapeDtypeStruct((m, d), F32),
        grid=(m // tm, nj),
        in_specs=[
            pl.BlockSpec((tm, d), lambda i, j: (i, 0)),
            pl.BlockSpec((1, d), lambda i, j: (0, 0)),
            pl.BlockSpec((1, 1, d), lambda i, j: ((i // tps) * 6 + 1, 0, 0)),
            pl.BlockSpec((1, 1, d), lambda i, j: ((i // tps) * 6 + 0, 0, 0)),
            pl.BlockSpec((d, tn), lambda i, j: (0, j)),
            pl.BlockSpec((d, tn), lambda i, j: (0, j + nj)),
            pl.BlockSpec((1, tn), lambda i, j: (0, j)),
            pl.BlockSpec((1, tn), lambda i, j: (0, j + nj)),
        ],
        out_specs=pl.BlockSpec((tm, tn), lambda i, j: (i, j)),
        scratch_shapes=[pltpu.VMEM((tm, d), BF16)],
        compiler_params=_params("parallel", "arbitrary"),
        name="conv_in_glu",
    )(x, ng.reshape(1, d), modv, modv, w_in, w_in, b2, b2)


def _heads_kernel(x_ref, ng_ref, sc_ref, sh_ref, w_ref, gv_ref, *rest, n_norm_tiles, hd, with_gate):
    if with_gate:
        wf_ref, bf_ref, o_ref, lf_ref, h_scr = rest
    else:
        o_ref, h_scr = rest
    j = pl.program_id(1)

    @pl.when(j == 0)
    def _():
        h = _normmod(x_ref[...], ng_ref[...], sc_ref[0], sh_ref[0]).astype(BF16)
        h_scr[...] = h
        if with_gate:
            lf_ref[...] = _log_sigmoid(_dot(h, wf_ref[...]) + bf_ref[...])

    r = _dot(h_scr[...], w_ref[...])
    tn = r.shape[1]

    @pl.when(j < n_norm_tiles)
    def _():
        for hh in range(tn // hd):
            sl = slice(hh * hd, (hh + 1) * hd)
            rh = r[:, sl]
            ms = jnp.mean(rh * rh, axis=-1, keepdims=True)
            o_ref[:, sl] = (rh * lax.rsqrt(ms + NORM_EPS) * gv_ref[:, sl]).astype(BF16)

    @pl.when(j >= n_norm_tiles)
    def _():
        o_ref[...] = r.astype(BF16)


def _heads_proj(x, ng, modv, w, gvec, n_norm_cols, hd, seq, wf=None, bf=None):
    m, d = x.shape
    n = w.shape[1]
    tm = _pick(seq, 1024, SUBLANES)
    tn = _pick(d, 512, LANES)
    tps = seq // tm
    with_gate = wf is not None
    in_specs = [
        pl.BlockSpec((tm, d), lambda i, j: (i, 0)),
        pl.BlockSpec((1, d), lambda i, j: (0, 0)),
        pl.BlockSpec((1, 1, d), lambda i, j: ((i // tps) * 6 + 1, 0, 0)),
        pl.BlockSpec((1, 1, d), lambda i, j: ((i // tps) * 6 + 0, 0, 0)),
        pl.BlockSpec((d, tn), lambda i, j: (0, j)),
        pl.BlockSpec((1, tn), lambda i, j: (0, j)),
    ]
    args = [x, ng.reshape(1, d), modv, modv, w, gvec]
    out_shape = [jax.ShapeDtypeStruct((m, n), BF16)]
    out_specs = [pl.BlockSpec((tm, tn), lambda i, j: (i, j))]
    if with_gate:
        in_specs += [pl.BlockSpec((d, LANES), lambda i, j: (0, 0)), pl.BlockSpec((1, LANES), lambda i, j: (0, 0))]
        args += [wf, bf]
        out_shape.append(jax.ShapeDtypeStruct((m, LANES), F32))
        out_specs.append(pl.BlockSpec((tm, LANES), lambda i, j: (i, 0)))
    return pl.pallas_call(
        functools.partial(_heads_kernel, n_norm_tiles=n_norm_cols // tn, hd=hd, with_gate=with_gate),
        out_shape=out_shape,
        grid=(m // tm, n // tn),
        in_specs=in_specs,
        out_specs=out_specs,
        scratch_shapes=[pltpu.VMEM((tm, d), BF16)],
        compiler_params=_params("parallel", "arbitrary"),
        name="heads_proj_gate" if with_gate else "heads_proj",
    )(*args)


CONV_HALO = 32
CONV_RC = 32
CONV_CC = 256


def _conv_out_kernel(u_ref, uh_ref, dww_ref, dwb_ref, lg_ref, lb_ref, w_ref, bo_ref, x_ref, g_ref,
                     o_ref, ubuf, cbuf, h_scr, *, width, tiles_per_seq):
    i = pl.program_id(0)
    j = pl.program_id(1)
    tm, d = u_ref.shape

    @pl.when(j == 0)
    def _():
        first = (i % tiles_per_seq) == 0
        ubuf[0:CONV_HALO, :] = jnp.where(first, 0.0, uh_ref[...])
        ubuf[CONV_HALO:CONV_HALO + tm, :] = u_ref[...]
        rc = min(CONV_RC, tm)
        cc = min(CONV_CC, d)
        base = CONV_HALO - (width - 1)

        def row_body(r, carry):
            r0 = pl.multiple_of(r * rc, rc)

            def col_body(c, carry2):
                c0 = pl.multiple_of(c * cc, cc)
                acc = jnp.zeros((rc, cc), F32) + dwb_ref[:, pl.ds(c0, cc)]
                win = ubuf[pl.ds(r0, rc + CONV_HALO), pl.ds(c0, cc)]
                for t in range(width):
                    acc = acc + win[base + t:base + t + rc, :] * dww_ref[t:t + 1, pl.ds(c0, cc)]
                cbuf[pl.ds(r0, rc), pl.ds(c0, cc)] = acc
                return carry2

            return lax.fori_loop(0, d // cc, col_body, carry)

        lax.fori_loop(0, tm // rc, row_body, 0)
        c = cbuf[...]
        mu = jnp.mean(c, axis=-1, keepdims=True)
        cen = c - mu
        var = jnp.mean(cen * cen, axis=-1, keepdims=True)
        y = cen * lax.rsqrt(var + NORM_EPS) * lg_ref[...] + lb_ref[...]
        h_scr[...] = (y * _sigmoid(y)).astype(BF16)

    y = _dot(h_scr[...], w_ref[...]) + bo_ref[...]
    o_ref[...] = x_ref[...] + (1.0 + g_ref[0]) * y


def _conv_out(u, x, modv, dw_w, dw_b, ln_g, ln_b, w_out, b_out, seq):
    m, d = u.shape
    width = dw_w.shape[0]
    assert width - 1 <= CONV_HALO
    tm = _pick(seq, 512, CONV_HALO)
    tn = _pick(d, 512, LANES)
    tps = seq // tm
    hb = tm // CONV_HALO
    wpad = -(-width // SUBLANES) * SUBLANES
    dww = jnp.pad(dw_w, ((0, wpad - width), (0, 0)))
    return pl.pallas_call(
        functools.partial(_conv_out_kernel, width=width, tiles_per_seq=tps),
        out_shape=jax.ShapeDtypeStruct((m, d), F32),
        grid=(m // tm, d // tn),
        in_specs=[
            pl.BlockSpec((tm, d), lambda i, j: (i, 0)),
            pl.BlockSpec((CONV_HALO, d), lambda i, j: (jnp.maximum(i * hb - 1, 0), 0)),
            pl.BlockSpec((wpad, d), lambda i, j: (0, 0)),
            pl.BlockSpec((1, d), lambda i, j: (0, 0)),
            pl.BlockSpec((1, d), lambda i, j: (0, 0)),
            pl.BlockSpec((1, d), lambda i, j: (0, 0)),
            pl.BlockSpec((d, tn), lambda i, j: (0, j)),
            pl.BlockSpec((1, tn), lambda i, j: (0, j)),
            pl.BlockSpec((tm, tn), lambda i, j: (i, j)),
            pl.BlockSpec((1, 1, tn), lambda i, j: ((i // tps) * 6 + 2, 0, j)),
        ],
        out_specs=pl.BlockSpec((tm, tn), lambda i, j: (i, j)),
        scratch_shapes=[
            pltpu.VMEM((tm + CONV_HALO, d), F32),
            pltpu.VMEM((tm, d), F32),
            pltpu.VMEM((tm, d), BF16),
        ],
        compiler_params=_params("parallel", "arbitrary"),
        name="conv_out",
    )(u, u, dww, dw_b.reshape(1, d), ln_g.reshape(1, d), ln_b.reshape(1, d), w_out, b_out.reshape(1, d), x, modv)


def _oproj_kernel(a_ref, w_ref, x_ref, g_ref, o_ref):
    y = _dot(a_ref[...], w_ref[...])
    o_ref[...] = x_ref[...] + (1.0 + g_ref[0]) * y


def _oproj_gated_kernel(a_ref, gt_ref, w_ref, x_ref, g_ref, o_ref, h_scr):
    @pl.when(pl.program_id(1) == 0)
    def _():
        h_scr[...] = (a_ref[...].astype(F32) * _sigmoid(gt_ref[...].astype(F32))).astype(BF16)

    y = _dot(h_scr[...], w_ref[...])
    o_ref[...] = x_ref[...] + (1.0 + g_ref[0]) * y


def _oproj(a, w, x, modv, seq, gate_src=None, gate_block=0):
    m, d = x.shape
    tm = _pick(seq, 1024, SUBLANES)
    tn = _pick(d, 512, LANES)
    tps = seq // tm
    common = [
        pl.BlockSpec((d, tn), lambda i, j: (0, j)),
        pl.BlockSpec((tm, tn), lambda i, j: (i, j)),
        pl.BlockSpec((1, 1, tn), lambda i, j: ((i // tps) * 6 + 2, 0, j)),
    ]
    if gate_src is None:
        kern, in_specs, args, scratch = _oproj_kernel, [pl.BlockSpec((tm, d), lambda i, j: (i, 0))], [a], []
        name = "attn_oproj"
    else:
        kern = _oproj_gated_kernel
        in_specs = [pl.BlockSpec((tm, d), lambda i, j: (i, 0)),
                    pl.BlockSpec((tm, d), lambda i, j: (i, gate_block))]
        args = [a, gate_src]
        scratch = [pltpu.VMEM((tm, d), BF16)]
        name = "attn_oproj_gated"
    return pl.pallas_call(
        kern,
        out_shape=jax.ShapeDtypeStruct((m, d), F32),
        grid=(m // tm, d // tn),
        in_specs=in_specs + common,
        out_specs=pl.BlockSpec((tm, tn), lambda i, j: (i, j)),
        scratch_shapes=scratch,
        compiler_params=_params("parallel", "arbitrary"),
        name=name,
    )(*args, w, x, modv)


MLP_OC = 512


def _mlp_kernel(x_ref, ng_ref, sc_ref, sh_ref, g_ref, w1_ref, w2_ref, o_ref, h_scr):
    f = pl.program_id(1)
    d = o_ref.shape[1]

    @pl.when(f == 0)
    def _():
        h_scr[...] = _normmod(x_ref[...], ng_ref[...], sc_ref[0], sh_ref[0]).astype(BF16)
        o_ref[...] = jnp.zeros_like(o_ref)

    u = _dot(h_scr[...], w1_ref[...])
    u = jnp.square(jnp.maximum(u, 0.0)).astype(BF16)
    oc = min(MLP_OC, d)
    for c in range(d // oc):
        sl = slice(c * oc, (c + 1) * oc)
        o_ref[:, sl] += _dot(u, w2_ref[:, sl])

    @pl.when(f == pl.num_programs(1) - 1)
    def _():
        o_ref[...] = x_ref[...] + (1.0 + g_ref[0]) * o_ref[...]


def _mlp(x, ng, modv, w1, w2, seq):
    m, d = x.shape
    dff = w1.shape[1]
    tm = _pick(seq, 512, SUBLANES)
    tf = _pick(dff, 1024, LANES)
    tps = seq // tm
    return pl.pallas_call(
        _mlp_kernel,
        out_shape=jax.ShapeDtypeStruct((m, d), F32),
        grid=(m // tm, dff // tf),
        in_specs=[
            pl.BlockSpec((tm, d), lambda i, f: (i, 0)),
            pl.BlockSpec((1, d), lambda i, f: (0, 0)),
            pl.BlockSpec((1, 1, d), lambda i, f: ((i // tps) * 6 + 4, 0, 0)),
            pl.BlockSpec((1, 1, d), lambda i, f: ((i // tps) * 6 + 3, 0, 0)),
            pl.BlockSpec((1, 1, d), lambda i, f: ((i // tps) * 6 + 5, 0, 0)),
            pl.BlockSpec((d, tf), lambda i, f: (0, f)),
            pl.BlockSpec((tf, d), lambda i, f: (f, 0)),
        ],
        out_specs=pl.BlockSpec((tm, d), lambda i, f: (i, 0)),
        scratch_shapes=[pltpu.VMEM((tm, d), BF16)],
        compiler_params=_params("parallel", "arbitrary"),
        name="mlp_fused",
    )(x, ng.reshape(1, d), modv, modv, modv, w1, w2)


def _cumsum_kernel(x_ref, u_ref, e_ref, m_ref, o_ref):
    hi = lax.Precision.HIGHEST
    x = x_ref[0]
    xc = jnp.dot(x, u_ref[...], precision=hi, preferred_element_type=F32)
    tot = jnp.dot(xc, e_ref[...], precision=hi, preferred_element_type=F32)
    o_ref[0] = xc + jnp.dot(m_ref[...], tot, precision=hi, preferred_element_type=F32)


def _cumsum_seq(lf):
    b, h, s = lf.shape
    nblk = s // LANES
    r = h * nblk
    x = lf.reshape(b, r, LANES)
    li = jnp.arange(LANES)
    upper = (li[:, None] <= li[None, :]).astype(F32)
    last = jnp.broadcast_to((li[:, None] == LANES - 1), (LANES, LANES)).astype(F32)
    ri = jnp.arange(r)
    prev = ((ri[:, None] // nblk == ri[None, :] // nblk) & (ri[None, :] < ri[:, None])).astype(F32)
    out = pl.pallas_call(
        _cumsum_kernel,
        out_shape=jax.ShapeDtypeStruct((b, r, LANES), F32),
        grid=(b,),
        in_specs=[
            pl.BlockSpec((1, r, LANES), lambda i: (i, 0, 0)),
            pl.BlockSpec((LANES, LANES), lambda i: (0, 0)),
            pl.BlockSpec((LANES, LANES), lambda i: (0, 0)),
            pl.BlockSpec((r, r), lambda i: (0, 0)),
        ],
        out_specs=pl.BlockSpec((1, r, LANES), lambda i: (i, 0, 0)),
        compiler_params=_params("parallel"),
        name="fox_cumsum",
    )(x, upper, last, prev)
    return out.reshape(b, h, s)


def _softmax_step(s, v, m_scr, l_scr, acc_scr):
    reps = s.shape[1] // LANES
    m_prev = m_scr[...]
    m_next = jnp.maximum(m_prev, jnp.max(s, axis=-1, keepdims=True))
    alpha = jnp.exp(m_prev - m_next)
    p = jnp.exp(s - jnp.tile(m_next, (1, reps)))
    l_scr[...] = alpha * l_scr[...] + jnp.sum(p, axis=-1, keepdims=True)
    acc_scr[...] = alpha * acc_scr[...] + _dot(p.astype(BF16), v)
    m_scr[...] = m_next


def _fox_attn_kernel(q_ref, k_ref, v_ref, cr_ref, ct_ref, o_ref, m_scr, l_scr, acc_scr, *, tq, scale):
    h = pl.program_id(1)
    i = pl.program_id(2)
    q = q_ref[0]
    lane = lax.broadcasted_iota(jnp.int32, (tq, LANES), 1)
    cq = jnp.sum(jnp.where(lane == h, ct_ref[0], 0.0), axis=-1, keepdims=True)
    m_scr[...] = jnp.full(m_scr.shape, NEG_INF, F32)
    l_scr[...] = jnp.zeros(l_scr.shape, F32)
    acc_scr[...] = jnp.zeros(acc_scr.shape, F32)

    def scores(j):
        k0 = pl.multiple_of(j * tq, tq)
        ks = k_ref[0, pl.ds(k0, tq), :]
        vs = v_ref[0, pl.ds(k0, tq), :]
        ck = cr_ref[0, 0, :, pl.ds(k0, tq)]
        s = _dot_t(q, ks) * scale + (cq - ck)
        return s, vs

    def body(j, carry):
        s, vs = scores(j)
        _softmax_step(s, vs, m_scr, l_scr, acc_scr)
        return carry

    lax.fori_loop(0, i, body, 0)
    s, vs = scores(i)
    row = lax.broadcasted_iota(jnp.int32, (tq, tq), 0)
    col = lax.broadcasted_iota(jnp.int32, (tq, tq), 1)
    _softmax_step(jnp.where(col <= row, s, NEG_INF), vs, m_scr, l_scr, acc_scr)
    o_ref[0] = (acc_scr[...] / l_scr[...]).astype(BF16)


def _fox_attention(qkvg, cum, bsz, seq, nh, hd):
    d = nh * hd
    tq = _pick(seq, 512, LANES)
    arr = qkvg.reshape(bsz, seq, 4 * d)
    cum_row = cum.reshape(bsz, nh, 1, seq)
    cum_t = jnp.pad(cum.transpose(0, 2, 1), ((0, 0), (0, 0), (0, LANES - nh)))
    out = pl.pallas_call(
        functools.partial(_fox_attn_kernel, tq=tq, scale=hd ** -0.5),
        out_shape=jax.ShapeDtypeStruct((bsz, seq, d), BF16),
        grid=(bsz, nh, seq // tq),
        in_specs=[
            pl.BlockSpec((1, tq, hd), lambda b, h, i: (b, i, h)),
            pl.BlockSpec((1, seq, hd), lambda b, h, i: (b, 0, nh + h)),
            pl.BlockSpec((1, seq, hd), lambda b, h, i: (b, 0, 2 * nh + h)),
            pl.BlockSpec((1, 1, 1, seq), lambda b, h, i: (b, h, 0, 0)),
            pl.BlockSpec((1, tq, LANES), lambda b, h, i: (b, i, 0)),
        ],
        out_specs=pl.BlockSpec((1, tq, hd), lambda b, h, i: (b, i, h)),
        scratch_shapes=[pltpu.VMEM((tq, LANES), F32), pltpu.VMEM((tq, LANES), F32), pltpu.VMEM((tq, hd), F32)],
        compiler_params=_params("parallel", "parallel", "arbitrary"),
        name="fox_attention",
    )(arr, arr, arr, cum_row, cum_t)
    return out.reshape(bsz * seq, d)


def _moba_attn_kernel(q_ref, k_ref, v_ref, ind_ref, o_ref, km_scr, m_scr, l_scr, acc_scr, *, scale, topk):
    i = pl.program_id(2)
    blk = MOBA_BLOCK

    @pl.when(i == 0)
    def _():
        km_scr[...] = (_dot(ind_ref[...], k_ref[0]) * (1.0 / blk)).astype(BF16)

    q = q_ref[0]
    lane = lax.broadcasted_iota(jnp.int32, (blk, LANES), 1)
    lane_f = lane.astype(F32)
    valid = lane < i
    gate = jnp.where(valid, _dot_t(q, km_scr[...]), NEG_INF)
    picked = jnp.zeros((blk, LANES), F32)
    for _ in range(topk):
        mx = jnp.max(gate, axis=-1, keepdims=True)
        first = jnp.min(jnp.where(gate == mx, lane_f, float(LANES)), axis=-1, keepdims=True)
        one = lane_f == first
        picked = jnp.where(one, 1.0, picked)
        gate = jnp.where(one, -jnp.inf, gate)
    sel_bias = jnp.where((picked > 0.0) & valid, 0.0, NEG_INF)

    m_scr[...] = jnp.full(m_scr.shape, NEG_INF, F32)
    l_scr[...] = jnp.zeros(l_scr.shape, F32)
    acc_scr[...] = jnp.zeros(acc_scr.shape, F32)

    def kv(j):
        k0 = pl.multiple_of(j * blk, blk)
        return k_ref[0, pl.ds(k0, blk), :], v_ref[0, pl.ds(k0, blk), :]

    def body(j, carry):
        ks, vs = kv(j)
        bias = jnp.sum(jnp.where(lane == j, sel_bias, 0.0), axis=-1, keepdims=True)
        s = jnp.where(bias == 0.0, _dot_t(q, ks) * scale, NEG_INF)
        _softmax_step(s, vs, m_scr, l_scr, acc_scr)
        return carry

    lax.fori_loop(0, i, body, 0)
    ks, vs = kv(i)
    row = lax.broadcasted_iota(jnp.int32, (blk, blk), 0)
    col = lax.broadcasted_iota(jnp.int32, (blk, blk), 1)
    s = jnp.where(col <= row, _dot_t(q, ks) * scale, NEG_INF)
    _softmax_step(s, vs, m_scr, l_scr, acc_scr)
    o_ref[0] = (acc_scr[...] / l_scr[...]).astype(BF16)


def _moba_attention(qkv, bsz, seq, nh, hd):
    d = nh * hd
    blk = MOBA_BLOCK
    assert seq % blk == 0 and seq // blk <= LANES
    arr = qkv.reshape(bsz, seq, 3 * d)
    ind = (jnp.arange(LANES)[:, None] == (jnp.arange(seq) // blk)[None, :]).astype(BF16)
    out = pl.pallas_call(
        functools.partial(_moba_attn_kernel, scale=hd ** -0.5, topk=MOBA_TOPK),
        out_shape=jax.ShapeDtypeStruct((bsz, seq, d), BF16),
        grid=(bsz, nh, seq // blk),
        in_specs=[
            pl.BlockSpec((1, blk, hd), lambda b, h, i: (b, i, h)),
            pl.BlockSpec((1, seq, hd), lambda b, h, i: (b, 0, nh + h)),
            pl.BlockSpec((1, seq, hd), lambda b, h, i: (b, 0, 2 * nh + h)),
            pl.BlockSpec((LANES, seq), lambda b, h, i: (0, 0)),
        ],
        out_specs=pl.BlockSpec((1, blk, hd), lambda b, h, i: (b, i, h)),
        scratch_shapes=[pltpu.VMEM((LANES, hd), BF16), pltpu.VMEM((blk, LANES), F32),
                        pltpu.VMEM((blk, LANES), F32), pltpu.VMEM((blk, hd), F32)],
        compiler_params=_params("parallel", "parallel", "arbitrary"),
        name="moba_attention",
    )(arr, arr, arr, ind)
    return out.reshape(bsz * seq, d)


def kernel(x, c, norm1_g, norm2_g, mod_w, mod_b, mlp_w1, mlp_w2, conv_w_in, conv_b_in, conv_dw_w, conv_dw_b,
           conv_ln_g, conv_ln_b, conv_w_out, conv_b_out, moba_w_qkv, moba_q_g, moba_k_g, moba_w_o, fox_w_in,
           fox_b_f, fox_q_g, fox_k_g, fox_w_o):
    bsz, seq, d = x.shape
    depth = mod_w.shape[0]
    hd = moba_q_g.shape[-1]
    nh = d // hd
    assert hd == LANES and nh <= LANES

    mod = _mod_vectors(c, mod_w, mod_b)
    xf = x.reshape(bsz * seq, d)
    ones_d = jnp.ones((d,), F32)

    for i in range(depth):
        kind = i % N_MIXERS
        jx = i // N_MIXERS
        modv = mod[i].reshape(bsz * 6, 1, d)
        if kind == 0:
            u = _conv_in(xf, norm1_g[i], modv, conv_w_in[jx].astype(BF16), conv_b_in[jx], seq)
            xf = _conv_out(u, xf, modv, conv_dw_w[jx], conv_dw_b[jx], conv_ln_g[jx], conv_ln_b[jx],
                           conv_w_out[jx].astype(BF16), conv_b_out[jx], seq)
        elif kind == 1:
            gvec = jnp.concatenate([jnp.tile(moba_q_g[jx], nh), jnp.tile(moba_k_g[jx], nh), ones_d]).reshape(1, 3 * d)
            (qkv,) = _heads_proj(xf, norm1_g[i], modv, moba_w_qkv[jx].astype(BF16), gvec, 2 * d, hd, seq)
            o = _moba_attention(qkv, bsz, seq, nh, hd)
            xf = _oproj(o, moba_w_o[jx].astype(BF16), xf, modv, seq)
        else:
            w_in = fox_w_in[jx]
            gvec = jnp.concatenate([jnp.tile(fox_q_g[jx], nh), jnp.tile(fox_k_g[jx], nh), ones_d, ones_d]).reshape(1, 4 * d)
            wf = jnp.pad(w_in[:, 4 * d:], ((0, 0), (0, LANES - nh))).astype(BF16)
            bf = jnp.pad(fox_b_f[jx], (0, LANES - nh)).reshape(1, LANES)
            qkvg, lf = _heads_proj(xf, norm1_g[i], modv, w_in[:, :4 * d].astype(BF16), gvec, 2 * d, hd, seq, wf, bf)
            lf = lf[:, :nh].reshape(bsz, seq, nh).transpose(0, 2, 1)
            cum = _cumsum_seq(lf)
            o = _fox_attention(qkvg, cum, bsz, seq, nh, hd)
            xf = _oproj(o, fox_w_o[jx].astype(BF16), xf, modv, seq, gate_src=qkvg, gate_block=3)
        xf = _mlp(xf, norm2_g[i], modv, mlp_w1[i].astype(BF16), mlp_w2[i].astype(BF16), seq)
    return xf.reshape(bsz, seq, d)
```

```python
import functools
import math

import jax
import jax.numpy as jnp
from jax import lax
from jax.experimental import pallas as pl
from jax.experimental.pallas import tpu as pltpu

N_MIXERS = 3
MOBA_BLOCK = 256
MOBA_TOPK = 3
NORM_EPS = 1e-6
NEG_INF = -1e30

LANES = 128
SUBLANES = 8
BF16_SUBLANES = 16
VMEM_LIMIT = 56 * 1024 * 1024

LOG2E = math.log2(math.e)

BF16 = jnp.bfloat16
F32 = jnp.float32


def _pick(n, pref, mult):
    if n <= pref:
        return n
    t = (pref // mult) * mult
    while t > mult and n % t:
        t -= mult
    assert n % t == 0, (n, pref, mult)
    return t


def _params(*sem):
    return pltpu.CompilerParams(dimension_semantics=sem, vmem_limit_bytes=VMEM_LIMIT)


def _dot(a, b):
    return jnp.dot(a, b, preferred_element_type=F32)


def _dot_t(a, b):
    return lax.dot_general(a, b, (((1,), (1,)), ((), ())), preferred_element_type=F32)


def _sigmoid(x):
    return 1.0 / (1.0 + jnp.exp(-x))


def _log_sigmoid(x):
    return -(jnp.maximum(-x, 0.0) + jnp.log1p(jnp.exp(-jnp.abs(x))))


def _normmod(x, ng, sc, sh):
    ms = jnp.mean(x * x, axis=-1, keepdims=True)
    y = x * lax.rsqrt(ms + NORM_EPS) * ng
    return y * (1.0 + sc) + sh


def _mod_kernel(c_ref, w_ref, b_ref, o_ref):
    c = c_ref[...]
    cond = c * _sigmoid(c)
    o_ref[0] = _dot(cond, w_ref[0]) + b_ref[0]


def _mod_vectors(c, mod_w, mod_b):
    depth, d, n = mod_w.shape
    b = c.shape[0]
    bp = -(-b // SUBLANES) * SUBLANES
    cp = jnp.pad(c, ((0, bp - b), (0, 0)))
    tn = _pick(n, 1024, LANES)
    out = pl.pallas_call(
        _mod_kernel,
        out_shape=jax.ShapeDtypeStruct((depth, bp, n), F32),
        grid=(depth, n // tn),
        in_specs=[
            pl.BlockSpec((bp, d), lambda i, j: (0, 0)),
            pl.BlockSpec((1, d, tn), lambda i, j: (i, 0, j)),
            pl.BlockSpec((1, 1, tn), lambda i, j: (i, 0, j)),
        ],
        out_specs=pl.BlockSpec((1, bp, tn), lambda i, j: (i, 0, j)),
        compiler_params=_params("parallel", "parallel"),
        name="mod_vectors",
    )(cp, mod_w, mod_b.reshape(depth, 1, n))
    return out[:, :b]


def _glu_kernel(x_ref, ng_ref, sc_ref, sh_ref, wa_ref, wg_ref, ba_ref, bg_ref, o_ref, h_scr):
    @pl.when(pl.program_id(1) == 0)
    def _():
        h_scr[...] = _normmod(x_ref[...], ng_ref[...], sc_ref[0], sh_ref[0]).astype(BF16)

    h = h_scr[...]
    a = _dot(h, wa_ref[...]) + ba_ref[...]
    g = _dot(h, wg_ref[...]) + bg_ref[...]
    o_ref[...] = a * _sigmoid(g)


def _conv_in(x, ng, modv, w_in, b_in, seq):
    m, d = x.shape
    tm = _pick(seq, 1024, SUBLANES)
    tn = _pick(d, 512, LANES)
    nj = d // tn
    tps = seq // tm
    b2 = b_in.reshape(1, 2 * d)
    return pl.pallas_call(
        _glu_kernel,
        out_shape=jax.ShapeDtypeStruct((m, d), F32),
        grid=(m // tm, nj),
        in_specs=[
            pl.BlockSpec((tm, d), lambda i, j: (i, 0)),
            pl.BlockSpec((1, d), lambda i, j: (0, 0)),
            pl.BlockSpec((1, 1, d), lambda i, j: ((i // tps) * 6 + 1, 0, 0)),
            pl.BlockSpec((1, 1, d), lambda i, j: ((i // tps) * 6 + 0, 0, 0)),
            pl.BlockSpec((d, tn), lambda i, j: (0, j)),
            pl.BlockSpec((d, tn), lambda i, j: (0, j + nj)),
            pl.BlockSpec((1, tn), lambda i, j: (0, j)),
            pl.BlockSpec((1, tn), lambda i, j: (0, j + nj)),
        ],
        out_specs=pl.BlockSpec((tm, tn), lambda i, j: (i, j)),
        scratch_shapes=[pltpu.VMEM((tm, d), BF16)],
        compiler_params=_params("parallel", "arbitrary"),
        name="conv_in_glu",
    )(x, ng.reshape(1, d), modv, modv, w_in, w_in, b2, b2)


def _heads_kernel(x_ref, ng_ref, sc_ref, sh_ref, w_ref, gv_ref, *rest, n_norm_tiles, hd, with_gate):
    if with_gate:
        wf_ref, bf_ref, o_ref, lf_ref, h_scr = rest
    else:
        o_ref, h_scr = rest
    j = pl.program_id(1)

    @pl.when(j == 0)
    def _():
        h = _normmod(x_ref[...], ng_ref[...], sc_ref[0], sh_ref[0]).astype(BF16)
        h_scr[...] = h
        if with_gate:
            lf_ref[...] = _log_sigmoid(_dot(h, wf_ref[...]) + bf_ref[...])

    r = _dot(h_scr[...], w_ref[...])
    tn = r.shape[1]

    @pl.when(j < n_norm_tiles)
    def _():
        for hh in range(tn // hd):
            sl = slice(hh * hd, (hh + 1) * hd)
            rh = r[:, sl]
            ms = jnp.mean(rh * rh, axis=-1, keepdims=True)
            o_ref[:, sl] = (rh * lax.rsqrt(ms + NORM_EPS) * gv_ref[:, sl]).astype(BF16)

    @pl.when(j >= n_norm_tiles)
    def _():
        o_ref[...] = r.astype(BF16)


def _heads_proj(x, ng, modv, w, gvec, n_norm_cols, hd, seq, wf=None, bf=None):
    m, d = x.shape
    n = w.shape[1]
    tm = _pick(seq, 1024, SUBLANES)
    tn = _pick(d, 512, LANES)
    tps = seq // tm
    with_gate = wf is not None
    in_specs = [
        pl.BlockSpec((tm, d), lambda i, j: (i, 0)),
        pl.BlockSpec((1, d), lambda i, j: (0, 0)),
        pl.BlockSpec((1, 1, d), lambda i, j: ((i // tps) * 6 + 1, 0, 0)),
        pl.BlockSpec((1, 1, d), lambda i, j: ((i // tps) * 6 + 0, 0, 0)),
        pl.BlockSpec((d, tn), lambda i, j: (0, j)),
        pl.BlockSpec((1, tn), lambda i, j: (0, j)),
    ]
    args = [x, ng.reshape(1, d), modv, modv, w, gvec]
    out_shape = [jax.ShapeDtypeStruct((m, n), BF16)]
    out_specs = [pl.BlockSpec((tm, tn), lambda i, j: (i, j))]
    if with_gate:
        in_specs += [pl.BlockSpec((d, LANES), lambda i, j: (0, 0)), pl.BlockSpec((1, LANES), lambda i, j: (0, 0))]
        args += [wf, bf]
        out_shape.append(jax.ShapeDtypeStruct((m, LANES), F32))
        out_specs.append(pl.BlockSpec((tm, LANES), lambda i, j: (i, 0)))
    return pl.pallas_call(
        functools.partial(_heads_kernel, n_norm_tiles=n_norm_cols // tn, hd=hd, with_gate=with_gate),
        out_shape=out_shape,
        grid=(m // tm, n // tn),
        in_specs=in_specs,
        out_specs=out_specs,
        scratch_shapes=[pltpu.VMEM((tm, d), BF16)],
        compiler_params=_params("parallel", "arbitrary"),
        name="heads_proj_gate" if with_gate else "heads_proj",
    )(*args)


CONV_HALO = 32
CONV_RC = 32
CONV_CC = 256


def _conv_out_kernel(u_ref, uh_ref, dww_ref, dwb_ref, lg_ref, lb_ref, w_ref, bo_ref, x_ref, g_ref,
                     o_ref, ubuf, cbuf, h_scr, *, width, tiles_per_seq):
    i = pl.program_id(0)
    j = pl.program_id(1)
    tm, d = u_ref.shape

    @pl.when(j == 0)
    def _():
        first = (i % tiles_per_seq) == 0
        ubuf[0:CONV_HALO, :] = jnp.where(first, 0.0, uh_ref[...])
        ubuf[CONV_HALO:CONV_HALO + tm, :] = u_ref[...]
        rc = min(CONV_RC, tm)
        cc = min(CONV_CC, d)
        base = CONV_HALO - (width - 1)

        def row_body(r, carry):
            r0 = pl.multiple_of(r * rc, rc)

            def col_body(c, carry2):
                c0 = pl.multiple_of(c * cc, cc)
                acc = jnp.zeros((rc, cc), F32) + dwb_ref[:, pl.ds(c0, cc)]
                win = ubuf[pl.ds(r0, rc + CONV_HALO), pl.ds(c0, cc)]
                rows = rc + CONV_HALO
                for res in range(SUBLANES):
                    sh = win if res == 0 else pltpu.roll(win, rows - res, axis=0)
                    for t in range(width):
                        if (base + t) % SUBLANES == res:
                            a0 = base + t - res
                            acc = acc + sh[a0:a0 + rc, :] * dww_ref[t:t + 1, pl.ds(c0, cc)]
                cbuf[pl.ds(r0, rc), pl.ds(c0, cc)] = acc
                return carry2

            return lax.fori_loop(0, d // cc, col_body, carry)

        lax.fori_loop(0, tm // rc, row_body, 0)
        c = cbuf[...]
        mu = jnp.mean(c, axis=-1, keepdims=True)
        cen = c - mu
        var = jnp.mean(cen * cen, axis=-1, keepdims=True)
        y = cen * lax.rsqrt(var + NORM_EPS) * lg_ref[...] + lb_ref[...]
        h_scr[...] = (y * _sigmoid(y)).astype(BF16)

    y = _dot(h_scr[...], w_ref[...]) + bo_ref[...]
    o_ref[...] = x_ref[...] + (1.0 + g_ref[0]) * y


def _conv_out(u, x, modv, dw_w, dw_b, ln_g, ln_b, w_out, b_out, seq):
    m, d = u.shape
    width = dw_w.shape[0]
    assert width - 1 <= CONV_HALO
    tm = _pick(seq, 512, CONV_HALO)
    tn = _pick(d, 512, LANES)
    tps = seq // tm
    hb = tm // CONV_HALO
    wpad = -(-width // SUBLANES) * SUBLANES
    dww = jnp.pad(dw_w, ((0, wpad - width), (0, 0)))
    return pl.pallas_call(
        functools.partial(_conv_out_kernel, width=width, tiles_per_seq=tps),
        out_shape=jax.ShapeDtypeStruct((m, d), F32),
        grid=(m // tm, d // tn),
        in_specs=[
            pl.BlockSpec((tm, d), lambda i, j: (i, 0)),
            pl.BlockSpec((CONV_HALO, d), lambda i, j: (jnp.maximum(i * hb - 1, 0), 0)),
            pl.BlockSpec((wpad, d), lambda i, j: (0, 0)),
            pl.BlockSpec((1, d), lambda i, j: (0, 0)),
            pl.BlockSpec((1, d), lambda i, j: (0, 0)),
            pl.BlockSpec((1, d), lambda i, j: (0, 0)),
            pl.BlockSpec((d, tn), lambda i, j: (0, j)),
            pl.BlockSpec((1, tn), lambda i, j: (0, j)),
            pl.BlockSpec((tm, tn), lambda i, j: (i, j)),
            pl.BlockSpec((1, 1, tn), lambda i, j: ((i // tps) * 6 + 2, 0, j)),
        ],
        out_specs=pl.BlockSpec((tm, tn), lambda i, j: (i, j)),
        scratch_shapes=[
            pltpu.VMEM((tm + CONV_HALO, d), F32),
            pltpu.VMEM((tm, d), F32),
            pltpu.VMEM((tm, d), BF16),
        ],
        compiler_params=_params("parallel", "arbitrary"),
        name="conv_out",
    )(u, u, dww, dw_b.reshape(1, d), ln_g.reshape(1, d), ln_b.reshape(1, d), w_out, b_out.reshape(1, d), x, modv)


def _oproj_kernel(a_ref, w_ref, x_ref, g_ref, o_ref):
    y = _dot(a_ref[...], w_ref[...])
    o_ref[...] = x_ref[...] + (1.0 + g_ref[0]) * y


def _oproj_gated_kernel(a_ref, gt_ref, w_ref, x_ref, g_ref, o_ref, h_scr):
    @pl.when(pl.program_id(1) == 0)
    def _():
        h_scr[...] = (a_ref[...].astype(F32) * _sigmoid(gt_ref[...].astype(F32))).astype(BF16)

    y = _dot(h_scr[...], w_ref[...])
    o_ref[...] = x_ref[...] + (1.0 + g_ref[0]) * y


def _oproj(a, w, x, modv, seq, gate_src=None, gate_block=0):
    m, d = x.shape
    tm = _pick(seq, 1024, SUBLANES)
    tn = _pick(d, 512, LANES)
    tps = seq // tm
    common = [
        pl.BlockSpec((d, tn), lambda i, j: (0, j)),
        pl.BlockSpec((tm, tn), lambda i, j: (i, j)),
        pl.BlockSpec((1, 1, tn), lambda i, j: ((i // tps) * 6 + 2, 0, j)),
    ]
    if gate_src is None:
        kern, in_specs, args, scratch = _oproj_kernel, [pl.BlockSpec((tm, d), lambda i, j: (i, 0))], [a], []
        name = "attn_oproj"
    else:
        kern = _oproj_gated_kernel
        in_specs = [pl.BlockSpec((tm, d), lambda i, j: (i, 0)),
                    pl.BlockSpec((tm, d), lambda i, j: (i, gate_block))]
        args = [a, gate_src]
        scratch = [pltpu.VMEM((tm, d), BF16)]
        name = "attn_oproj_gated"
    return pl.pallas_call(
        kern,
        out_shape=jax.ShapeDtypeStruct((m, d), F32),
        grid=(m // tm, d // tn),
        in_specs=in_specs + common,
        out_specs=pl.BlockSpec((tm, tn), lambda i, j: (i, j)),
        scratch_shapes=scratch,
        compiler_params=_params("parallel", "arbitrary"),
        name=name,
    )(*args, w, x, modv)


MLP_OC = 512


def _mlp_kernel(x_ref, ng_ref, sc_ref, sh_ref, g_ref, w1_ref, w2_ref, o_ref, h_scr):
    f = pl.program_id(1)
    d = o_ref.shape[1]

    @pl.when(f == 0)
    def _():
        h_scr[...] = _normmod(x_ref[...], ng_ref[...], sc_ref[0], sh_ref[0]).astype(BF16)
        o_ref[...] = jnp.zeros_like(o_ref)

    u = _dot(h_scr[...], w1_ref[...])
    u = jnp.square(jnp.maximum(u, 0.0)).astype(BF16)
    oc = min(MLP_OC, d)
    for c in range(d // oc):
        sl = slice(c * oc, (c + 1) * oc)
        o_ref[:, sl] += _dot(u, w2_ref[:, sl])

    @pl.when(f == pl.num_programs(1) - 1)
    def _():
        o_ref[...] = x_ref[...] + (1.0 + g_ref[0]) * o_ref[...]


def _mlp(x, ng, modv, w1, w2, seq):
    m, d = x.shape
    dff = w1.shape[1]
    tm = _pick(seq, 512, SUBLANES)
    tf = _pick(dff, 1024, LANES)
    tps = seq // tm
    return pl.pallas_call(
        _mlp_kernel,
        out_shape=jax.ShapeDtypeStruct((m, d), F32),
        grid=(m // tm, dff // tf),
        in_specs=[
            pl.BlockSpec((tm, d), lambda i, f: (i, 0)),
            pl.BlockSpec((1, d), lambda i, f: (0, 0)),
            pl.BlockSpec((1, 1, d), lambda i, f: ((i // tps) * 6 + 4, 0, 0)),
            pl.BlockSpec((1, 1, d), lambda i, f: ((i // tps) * 6 + 3, 0, 0)),
            pl.BlockSpec((1, 1, d), lambda i, f: ((i // tps) * 6 + 5, 0, 0)),
            pl.BlockSpec((d, tf), lambda i, f: (0, f)),
            pl.BlockSpec((tf, d), lambda i, f: (f, 0)),
        ],
        out_specs=pl.BlockSpec((tm, d), lambda i, f: (i, 0)),
        scratch_shapes=[pltpu.VMEM((tm, d), BF16)],
        compiler_params=_params("parallel", "arbitrary"),
        name="mlp_fused",
    )(x, ng.reshape(1, d), modv, modv, modv, w1, w2)


def _cumsum_kernel(x_ref, u_ref, e_ref, m_ref, o_ref):
    hi = lax.Precision.HIGHEST
    x = x_ref[0]
    xc = jnp.dot(x, u_ref[...], precision=hi, preferred_element_type=F32)
    tot = jnp.dot(xc, e_ref[...], precision=hi, preferred_element_type=F32)
    o_ref[0] = xc + jnp.dot(m_ref[...], tot, precision=hi, preferred_element_type=F32)


def _cumsum_seq(lf):
    b, h, s = lf.shape
    nblk = s // LANES
    r = h * nblk
    x = lf.reshape(b, r, LANES)
    li = jnp.arange(LANES)
    upper = (li[:, None] <= li[None, :]).astype(F32)
    last = jnp.broadcast_to((li[:, None] == LANES - 1), (LANES, LANES)).astype(F32)
    ri = jnp.arange(r)
    prev = ((ri[:, None] // nblk == ri[None, :] // nblk) & (ri[None, :] < ri[:, None])).astype(F32)
    out = pl.pallas_call(
        _cumsum_kernel,
        out_shape=jax.ShapeDtypeStruct((b, r, LANES), F32),
        grid=(b,),
        in_specs=[
            pl.BlockSpec((1, r, LANES), lambda i: (i, 0, 0)),
            pl.BlockSpec((LANES, LANES), lambda i: (0, 0)),
            pl.BlockSpec((LANES, LANES), lambda i: (0, 0)),
            pl.BlockSpec((r, r), lambda i: (0, 0)),
        ],
        out_specs=pl.BlockSpec((1, r, LANES), lambda i: (i, 0, 0)),
        compiler_params=_params("parallel"),
        name="fox_cumsum",
    )(x, upper, last, prev)
    return out.reshape(b, h, s)


HEADS_PER_STEP = 2
KW = 2 * LANES
VT_ROWS = LANES + BF16_SUBLANES


def _split3(c):
    c1 = c.astype(BF16).astype(F32)
    r1 = c - c1
    c2 = r1.astype(BF16).astype(F32)
    c3 = (r1 - c2).astype(BF16).astype(F32)
    return c1, c2, c3


def _lane_column(tile, h):
    lane = lax.broadcasted_iota(jnp.int32, tile.shape, 1)
    return jnp.sum(jnp.where(lane == h, tile, 0.0), axis=-1, keepdims=True)


def _kv_prep_kernel(*refs, fox):
    if fox:
        k_ref, v_ref, ct_ref, ka_ref, vt_ref = refs
    else:
        k_ref, v_ref, ka_ref, vt_ref = refs
    hp = pl.program_id(1)
    i = pl.program_id(2)
    ts = k_ref.shape[1]
    lane = lax.broadcasted_iota(jnp.int32, (ts, LANES), 1)
    for hh in range(HEADS_PER_STEP):
        sl = slice(hh * LANES, (hh + 1) * LANES)
        if fox:
            ck = _lane_column(ct_ref[0], hp * HEADS_PER_STEP + hh) * LOG2E
            c1, c2, c3 = _split3(ck)
            extra = jnp.where(lane == 0, -c1, jnp.where(lane == 1, -c2, jnp.where(lane == 2, -c3,
                              jnp.where(lane < 6, 1.0, 0.0))))
        else:
            row = lax.broadcasted_iota(jnp.int32, (ts, LANES), 0) + i * ts
            extra = jnp.where(lane == lax.shift_right_logical(row, int(math.log2(MOBA_BLOCK))), 1.0, 0.0)
        ka_ref[0, hh, :, 0:LANES] = k_ref[0, :, sl]
        ka_ref[0, hh, :, LANES:KW] = extra.astype(BF16)
        sub = lax.broadcasted_iota(jnp.int32, (BF16_SUBLANES, ts), 0)
        vt_ref[0, hh, 0:LANES, :] = v_ref[0, :, sl].astype(F32).T.astype(BF16)
        vt_ref[0, hh, LANES:VT_ROWS, :] = jnp.where(sub == 0, 1.0, 0.0).astype(BF16)


def _kv_prep(arr, k_blk, v_blk, cum_t, bsz, seq, nh):
    fox = cum_t is not None
    ts = _pick(seq, 1024, LANES)
    hps = HEADS_PER_STEP
    in_specs = [
        pl.BlockSpec((1, ts, hps * LANES), lambda b, h, i: (b, i, k_blk + h)),
        pl.BlockSpec((1, ts, hps * LANES), lambda b, h, i: (b, i, v_blk + h)),
    ]
    args = [arr, arr]
    if fox:
        in_specs.append(pl.BlockSpec((1, ts, LANES), lambda b, h, i: (b, i, 0)))
        args.append(cum_t)
    return pl.pallas_call(
        functools.partial(_kv_prep_kernel, fox=fox),
        out_shape=[jax.ShapeDtypeStruct((bsz, nh, seq, KW), BF16),
                   jax.ShapeDtypeStruct((bsz, nh, VT_ROWS, seq), BF16)],
        grid=(bsz, nh // hps, seq // ts),
        in_specs=in_specs,
        out_specs=[pl.BlockSpec((1, hps, ts, KW), lambda b, h, i: (b, h, i, 0)),
                   pl.BlockSpec((1, hps, VT_ROWS, ts), lambda b, h, i: (b, h, 0, i))],
        compiler_params=_params("parallel", "parallel", "parallel"),
        name="fox_kv_prep" if fox else "moba_kv_prep",
    )(*args)


def _attn_kernel(*refs, fox, tq, nb_pad, topk):
    if fox:
        q_ref, ka_ref, vt_ref, ct_ref, o_ref, qa_scr, m_scr, acc_scr = refs
    else:
        q_ref, ka_ref, vt_ref, ind_ref, o_ref, qa_scr, m_scr, acc_scr, km_scr = refs
    hp = pl.program_id(1)
    i = pl.program_id(2)
    hps = HEADS_PER_STEP

    for hh in range(hps):
        q = q_ref[0, :, hh * LANES:(hh + 1) * LANES]
        if fox:
            lane = lax.broadcasted_iota(jnp.int32, (tq, LANES), 1)
            cq = _lane_column(ct_ref[0], hp * hps + hh) * LOG2E
            c1, c2, c3 = _split3(cq)
            extra = jnp.where(lane < 3, 1.0, jnp.where(lane == 3, c1, jnp.where(lane == 4, c2,
                              jnp.where(lane == 5, c3, 0.0))))
        else:
            @pl.when(i == 0)
            def _():
                km_scr[hh] = (_dot(ind_ref[...], ka_ref[0, hh, :, 0:LANES]) * (1.0 / MOBA_BLOCK)).astype(BF16)

            gate = _dot_t(km_scr[hh, 0:nb_pad, :], q)
            blk = lax.broadcasted_iota(jnp.int32, (nb_pad, tq), 0)
            blk_f = blk.astype(F32)
            qpos = lax.broadcasted_iota(jnp.int32, (nb_pad, tq), 1) + i * tq
            cur = lax.shift_right_logical(qpos, int(math.log2(MOBA_BLOCK)))
            valid = blk < cur
            gate = jnp.where(valid, gate, NEG_INF)
            picked = jnp.zeros((nb_pad, tq), F32)
            for _ in range(topk):
                mx = jnp.max(gate, axis=0, keepdims=True)
                first = jnp.min(jnp.where(gate == mx, blk_f, float(nb_pad)), axis=0, keepdims=True)
                one = blk_f == first
                picked = jnp.where(one, 1.0, picked)
                gate = jnp.where(one, -jnp.inf, gate)
            allowed = ((picked > 0.0) & valid) | (blk == cur)
            sel_t = jnp.where(allowed, 0.0, NEG_INF)
            if nb_pad < LANES:
                sel_t = jnp.concatenate([sel_t, jnp.zeros((LANES - nb_pad, tq), F32)], axis=0)
            extra = sel_t.T
        qa_scr[hh, :, 0:LANES] = q
        qa_scr[hh, :, LANES:KW] = extra.astype(BF16)
        m_scr[hh] = jnp.full((SUBLANES, tq), NEG_INF, F32)
        acc_scr[hh] = jnp.zeros((VT_ROWS, tq), F32)

    def step(j, masked):
        k0 = pl.multiple_of(j * tq, tq)
        scores = [_dot_t(ka_ref[0, hh, pl.ds(k0, tq), :], qa_scr[hh]) for hh in range(hps)]
        for hh in range(hps):
            vt = vt_ref[0, hh, :, pl.ds(k0, tq)]
            s = scores[hh]
            if masked:
                key = lax.broadcasted_iota(jnp.int32, (tq, tq), 0)
                qry = lax.broadcasted_iota(jnp.int32, (tq, tq), 1)
                s = jnp.where(key <= qry, s, NEG_INF)
            m_prev = m_scr[hh, 0:1, :]
            m_new = jnp.maximum(m_prev, jnp.max(s, axis=0, keepdims=True))
            alpha = jnp.exp2(m_prev - m_new)
            p = jnp.exp2(s - m_new).astype(BF16)
            acc_scr[hh] = alpha * acc_scr[hh] + _dot(vt, p)
            m_scr[hh] = jnp.broadcast_to(m_new, (SUBLANES, tq))

    def body(j, carry):
        step(j, False)
        return carry

    lax.fori_loop(0, i, body, 0)
    step(i, True)

    for hh in range(hps):
        acc = acc_scr[hh]
        o_t = acc[0:LANES, :] / acc[LANES:LANES + 1, :]
        o_ref[0, :, hh * LANES:(hh + 1) * LANES] = o_t.T.astype(BF16)


def _attention(arr, q_blk, ka, vt, cum_t, bsz, seq, nh):
    fox = cum_t is not None
    hps = HEADS_PER_STEP
    d = nh * LANES
    tq = _pick(seq, 512, LANES)
    nb = seq // MOBA_BLOCK
    nb_pad = -(-nb // SUBLANES) * SUBLANES
    in_specs = [
        pl.BlockSpec((1, tq, hps * LANES), lambda b, h, i: (b, i, q_blk + h)),
        pl.BlockSpec((1, hps, seq, KW), lambda b, h, i: (b, h, 0, 0)),
        pl.BlockSpec((1, hps, VT_ROWS, seq), lambda b, h, i: (b, h, 0, 0)),
    ]
    args = [arr, ka, vt]
    scratch = [pltpu.VMEM((hps, tq, KW), BF16), pltpu.VMEM((hps, SUBLANES, tq), F32),
               pltpu.VMEM((hps, VT_ROWS, tq), F32)]
    if fox:
        in_specs.append(pl.BlockSpec((1, tq, LANES), lambda b, h, i: (b, i, 0)))
        args.append(cum_t)
    else:
        assert seq % MOBA_BLOCK == 0 and nb_pad <= LANES and tq % MOBA_BLOCK == 0
        ind = (jnp.arange(LANES)[:, None] == (jnp.arange(seq) // MOBA_BLOCK)[None, :]).astype(BF16)
        in_specs.append(pl.BlockSpec((LANES, seq), lambda b, h, i: (0, 0)))
        args.append(ind)
        scratch.append(pltpu.VMEM((hps, LANES, LANES), BF16))
    out = pl.pallas_call(
        functools.partial(_attn_kernel, fox=fox, tq=tq, nb_pad=nb_pad, topk=MOBA_TOPK),
        out_shape=jax.ShapeDtypeStruct((bsz, seq, d), BF16),
        grid=(bsz, nh // hps, seq // tq),
        in_specs=in_specs,
        out_specs=pl.BlockSpec((1, tq, hps * LANES), lambda b, h, i: (b, i, h)),
        scratch_shapes=scratch,
        compiler_params=_params("parallel", "parallel", "arbitrary"),
        name="fox_attention" if fox else "moba_attention",
    )(*args)
    return out.reshape(bsz * seq, d)


def kernel(x, c, norm1_g, norm2_g, mod_w, mod_b, mlp_w1, mlp_w2, conv_w_in, conv_b_in, conv_dw_w, conv_dw_b,
           conv_ln_g, conv_ln_b, conv_w_out, conv_b_out, moba_w_qkv, moba_q_g, moba_k_g, moba_w_o, fox_w_in,
           fox_b_f, fox_q_g, fox_k_g, fox_w_o):
    bsz, seq, d = x.shape
    depth = mod_w.shape[0]
    hd = moba_q_g.shape[-1]
    nh = d // hd
    assert hd == LANES and nh <= LANES and nh % HEADS_PER_STEP == 0
    hps = HEADS_PER_STEP
    qscale = hd ** -0.5 * LOG2E

    mod = _mod_vectors(c, mod_w, mod_b)
    xf = x.reshape(bsz * seq, d)
    ones_d = jnp.ones((d,), F32)

    for i in range(depth):
        kind = i % N_MIXERS
        jx = i // N_MIXERS
        modv = mod[i].reshape(bsz * 6, 1, d)
        if kind == 0:
            u = _conv_in(xf, norm1_g[i], modv, conv_w_in[jx].astype(BF16), conv_b_in[jx], seq)
            xf = _conv_out(u, xf, modv, conv_dw_w[jx], conv_dw_b[jx], conv_ln_g[jx], conv_ln_b[jx],
                           conv_w_out[jx].astype(BF16), conv_b_out[jx], seq)
        elif kind == 1:
            gvec = jnp.concatenate([jnp.tile(moba_q_g[jx], nh) * qscale, jnp.tile(moba_k_g[jx], nh),
                                    ones_d]).reshape(1, 3 * d)
            (qkv,) = _heads_proj(xf, norm1_g[i], modv, moba_w_qkv[jx].astype(BF16), gvec, 2 * d, hd, seq)
            arr = qkv.reshape(bsz, seq, 3 * d)
            ka, vt = _kv_prep(arr, nh // hps, 2 * nh // hps, None, bsz, seq, nh)
            o = _attention(arr, 0, ka, vt, None, bsz, seq, nh)
            xf = _oproj(o, moba_w_o[jx].astype(BF16), xf, modv, seq)
        else:
            w_in = fox_w_in[jx]
            gvec = jnp.concatenate([jnp.tile(fox_q_g[jx], nh) * qscale, jnp.tile(fox_k_g[jx], nh),
                                    ones_d, ones_d]).reshape(1, 4 * d)
            wf = jnp.pad(w_in[:, 4 * d:], ((0, 0), (0, LANES - nh))).astype(BF16)
            bf = jnp.pad(fox_b_f[jx], (0, LANES - nh)).reshape(1, LANES)
            qkvg, lf = _heads_proj(xf, norm1_g[i], modv, w_in[:, :4 * d].astype(BF16), gvec, 2 * d, hd, seq, wf, bf)
            lf = lf[:, :nh].reshape(bsz, seq, nh).transpose(0, 2, 1)
            cum = _cumsum_seq(lf)
            cum_t = jnp.pad(cum.transpose(0, 2, 1), ((0, 0), (0, 0), (0, LANES - nh)))
            arr = qkvg.reshape(bsz, seq, 4 * d)
            ka, vt = _kv_prep(arr, nh // hps, 2 * nh // hps, cum_t, bsz, seq, nh)
            o = _attention(arr, 0, ka, vt, cum_t, bsz, seq, nh)
            xf = _oproj(o, fox_w_o[jx].astype(BF16), xf, modv, seq, gate_src=qkvg, gate_block=3)
        xf = _mlp(xf, norm2_g[i], modv, mlp_w1[i].astype(BF16), mlp_w2[i].astype(BF16), seq)
    return xf.reshape(bsz, seq, d)
```

```python
import functools
import math

import jax
import jax.numpy as jnp
from jax import lax
from jax.experimental import pallas as pl
from jax.experimental.pallas import tpu as pltpu

N_MIXERS = 3
MOBA_BLOCK = 256
MOBA_TOPK = 3
NORM_EPS = 1e-6
NEG_INF = -1e30

LANES = 128
SUBLANES = 8
BF16_SUBLANES = 16
VMEM_LIMIT = 56 * 1024 * 1024

LOG2E = math.log2(math.e)

BF16 = jnp.bfloat16
F32 = jnp.float32


def _pick(n, pref, mult):
    if n <= pref:
        return n
    t = (pref // mult) * mult
    while t > mult and n % t:
        t -= mult
    assert n % t == 0, (n, pref, mult)
    return t


def _params(*sem):
    return pltpu.CompilerParams(dimension_semantics=sem, vmem_limit_bytes=VMEM_LIMIT)


def _dot(a, b):
    return jnp.dot(a, b, preferred_element_type=F32)


def _dot_t(a, b):
    return lax.dot_general(a, b, (((1,), (1,)), ((), ())), preferred_element_type=F32)


def _sigmoid(x):
    return 1.0 / (1.0 + jnp.exp(-x))


def _log_sigmoid(x):
    return -(jnp.maximum(-x, 0.0) + jnp.log1p(jnp.exp(-jnp.abs(x))))


NORM_ROWS = 64


def _normmod(x_ref, ng_ref, sc_ref, sh_ref, h_ref):
    tm = x_ref.shape[0]
    rows = min(NORM_ROWS, tm)
    ng = ng_ref[...]
    sc1 = 1.0 + sc_ref[0]
    sh = sh_ref[0]
    for c in range(tm // rows):
        sl = slice(c * rows, (c + 1) * rows)
        x = x_ref[sl, :]
        ms = jnp.mean(x * x, axis=-1, keepdims=True)
        y = x * lax.rsqrt(ms + NORM_EPS) * ng
        h_ref[sl, :] = (y * sc1 + sh).astype(h_ref.dtype)


def _mod_kernel(c_ref, w_ref, b_ref, o_ref):
    c = c_ref[...]
    cond = c * _sigmoid(c)
    o_ref[0] = _dot(cond, w_ref[0]) + b_ref[0]


def _mod_vectors(c, mod_w, mod_b):
    depth, d, n = mod_w.shape
    b = c.shape[0]
    bp = -(-b // SUBLANES) * SUBLANES
    cp = jnp.pad(c, ((0, bp - b), (0, 0)))
    tn = _pick(n, 1024, LANES)
    out = pl.pallas_call(
        _mod_kernel,
        out_shape=jax.ShapeDtypeStruct((depth, bp, n), F32),
        grid=(depth, n // tn),
        in_specs=[
            pl.BlockSpec((bp, d), lambda i, j: (0, 0)),
            pl.BlockSpec((1, d, tn), lambda i, j: (i, 0, j)),
            pl.BlockSpec((1, 1, tn), lambda i, j: (i, 0, j)),
        ],
        out_specs=pl.BlockSpec((1, bp, tn), lambda i, j: (i, 0, j)),
        compiler_params=_params("parallel", "parallel"),
        name="mod_vectors",
    )(cp, mod_w, mod_b.reshape(depth, 1, n))
    return out[:, :b]


PROJ_SUB = 512


def _glu_kernel(x_ref, ng_ref, sc_ref, sh_ref, wa_ref, wg_ref, ba_ref, bg_ref, o_ref, h_scr):
    @pl.when(pl.program_id(1) == 0)
    def _():
        _normmod(x_ref, ng_ref, sc_ref, sh_ref, h_scr)

    tn = o_ref.shape[1]
    sub = min(PROJ_SUB, tn)
    for c in range(tn // sub):
        sl = slice(c * sub, (c + 1) * sub)
        a = _dot(h_scr[...], wa_ref[:, sl]) + ba_ref[:, sl]
        g = _dot(h_scr[...], wg_ref[:, sl]) + bg_ref[:, sl]
        o_ref[:, sl] = a * _sigmoid(g)


def _conv_in(x, ng, modv, w_in, b_in, seq):
    m, d = x.shape
    tm = _pick(seq, 1024, SUBLANES)
    tn = _pick(d, 1024, LANES)
    nj = d // tn
    tps = seq // tm
    b2 = b_in.reshape(1, 2 * d)
    return pl.pallas_call(
        _glu_kernel,
        out_shape=jax.ShapeDtypeStruct((m, d), F32),
        grid=(m // tm, nj),
        in_specs=[
            pl.BlockSpec((tm, d), lambda i, j: (i, 0)),
            pl.BlockSpec((1, d), lambda i, j: (0, 0)),
            pl.BlockSpec((1, 1, d), lambda i, j: ((i // tps) * 6 + 1, 0, 0)),
            pl.BlockSpec((1, 1, d), lambda i, j: ((i // tps) * 6 + 0, 0, 0)),
            pl.BlockSpec((d, tn), lambda i, j: (0, j)),
            pl.BlockSpec((d, tn), lambda i, j: (0, j + nj)),
            pl.BlockSpec((1, tn), lambda i, j: (0, j)),
            pl.BlockSpec((1, tn), lambda i, j: (0, j + nj)),
        ],
        out_specs=pl.BlockSpec((tm, tn), lambda i, j: (i, j)),
        scratch_shapes=[pltpu.VMEM((tm, d), BF16)],
        compiler_params=_params("parallel", "arbitrary"),
        name="conv_in_glu",
    )(x, ng.reshape(1, d), modv, modv, w_in, w_in, b2, b2)


def _heads_kernel(x_ref, ng_ref, sc_ref, sh_ref, w_ref, *rest, norm, with_gate):
    rest = list(rest)
    gv_ref = rest.pop(0) if norm else None
    if with_gate:
        wf_ref, bf_ref, o_ref, lf_ref, h_scr = rest
    else:
        o_ref, h_scr = rest

    @pl.when(pl.program_id(1) == 0)
    def _():
        _normmod(x_ref, ng_ref, sc_ref, sh_ref, h_scr)
        if with_gate:
            lf_ref[...] = _log_sigmoid(_dot(h_scr[...], wf_ref[...]) + bf_ref[...])

    tn = o_ref.shape[1]
    sub = min(PROJ_SUB, tn)
    for c in range(tn // sub):
        r = _dot(h_scr[...], w_ref[:, c * sub:(c + 1) * sub])
        if norm:
            for hh in range(sub // LANES):
                sl = slice(c * sub + hh * LANES, c * sub + (hh + 1) * LANES)
                rh = r[:, hh * LANES:(hh + 1) * LANES]
                ms = jnp.mean(rh * rh, axis=-1, keepdims=True)
                o_ref[:, sl] = (rh * lax.rsqrt(ms + NORM_EPS) * gv_ref[:, sl]).astype(BF16)
        else:
            o_ref[:, c * sub:(c + 1) * sub] = r.astype(BF16)


def _heads_proj(x, ng, modv, w, col0, ncols, gvec, seq, wf=None, bf=None):
    m, d = x.shape
    tm = _pick(seq, 1024, SUBLANES)
    tn = _pick(ncols, 1024, LANES)
    assert col0 % tn == 0
    off = col0 // tn
    tps = seq // tm
    norm = gvec is not None
    with_gate = wf is not None
    in_specs = [
        pl.BlockSpec((tm, d), lambda i, j: (i, 0)),
        pl.BlockSpec((1, d), lambda i, j: (0, 0)),
        pl.BlockSpec((1, 1, d), lambda i, j: ((i // tps) * 6 + 1, 0, 0)),
        pl.BlockSpec((1, 1, d), lambda i, j: ((i // tps) * 6 + 0, 0, 0)),
        pl.BlockSpec((d, tn), lambda i, j: (0, j + off)),
    ]
    args = [x, ng.reshape(1, d), modv, modv, w]
    if norm:
        in_specs.append(pl.BlockSpec((1, tn), lambda i, j: (0, j)))
        args.append(gvec)
    out_shape = [jax.ShapeDtypeStruct((m, ncols), BF16)]
    out_specs = [pl.BlockSpec((tm, tn), lambda i, j: (i, j))]
    if with_gate:
        in_specs += [pl.BlockSpec((d, LANES), lambda i, j: (0, 0)), pl.BlockSpec((1, LANES), lambda i, j: (0, 0))]
        args += [wf, bf]
        out_shape.append(jax.ShapeDtypeStruct((m, LANES), F32))
        out_specs.append(pl.BlockSpec((tm, LANES), lambda i, j: (i, 0)))
    return pl.pallas_call(
        functools.partial(_heads_kernel, norm=norm, with_gate=with_gate),
        out_shape=out_shape,
        grid=(m // tm, ncols // tn),
        in_specs=in_specs,
        out_specs=out_specs,
        scratch_shapes=[pltpu.VMEM((tm, d), BF16)],
        compiler_params=_params("parallel", "arbitrary"),
        name=("heads_proj_norm" if norm else "heads_proj_plain") + ("_gate" if with_gate else ""),
    )(*args)


CONV_HALO = 32
CONV_RC = 32
CONV_CC = 256


def _conv_out_kernel(u_ref, uh_ref, dww_ref, dwb_ref, lg_ref, lb_ref, w_ref, bo_ref, x_ref, g_ref,
                     o_ref, ubuf, cbuf, h_scr, *, width, tiles_per_seq):
    i = pl.program_id(0)
    j = pl.program_id(1)
    tm, d = u_ref.shape

    @pl.when(j == 0)
    def _():
        first = (i % tiles_per_seq) == 0
        ubuf[0:CONV_HALO, :] = jnp.where(first, 0.0, uh_ref[...])
        ubuf[CONV_HALO:CONV_HALO + tm, :] = u_ref[...]
        rc = min(CONV_RC, tm)
        cc = min(CONV_CC, d)
        base = CONV_HALO - (width - 1)

        def row_body(r, carry):
            r0 = pl.multiple_of(r * rc, rc)

            def col_body(c, carry2):
                c0 = pl.multiple_of(c * cc, cc)
                acc = jnp.zeros((rc, cc), F32) + dwb_ref[:, pl.ds(c0, cc)]
                win = ubuf[pl.ds(r0, rc + CONV_HALO), pl.ds(c0, cc)]
                rows = rc + CONV_HALO
                for res in range(SUBLANES):
                    sh = win if res == 0 else pltpu.roll(win, rows - res, axis=0)
                    for t in range(width):
                        if (base + t) % SUBLANES == res:
                            a0 = base + t - res
                            acc = acc + sh[a0:a0 + rc, :] * dww_ref[t:t + 1, pl.ds(c0, cc)]
                cbuf[pl.ds(r0, rc), pl.ds(c0, cc)] = acc
                return carry2

            return lax.fori_loop(0, d // cc, col_body, carry)

        lax.fori_loop(0, tm // rc, row_body, 0)
        c = cbuf[...]
        mu = jnp.mean(c, axis=-1, keepdims=True)
        cen = c - mu
        var = jnp.mean(cen * cen, axis=-1, keepdims=True)
        y = cen * lax.rsqrt(var + NORM_EPS) * lg_ref[...] + lb_ref[...]
        h_scr[...] = (y * _sigmoid(y)).astype(BF16)

    y = _dot(h_scr[...], w_ref[...]) + bo_ref[...]
    o_ref[...] = x_ref[...] + (1.0 + g_ref[0]) * y


def _conv_out(u, x, modv, dw_w, dw_b, ln_g, ln_b, w_out, b_out, seq):
    m, d = u.shape
    width = dw_w.shape[0]
    assert width - 1 <= CONV_HALO
    tm = _pick(seq, 512, CONV_HALO)
    tn = _pick(d, 512, LANES)
    tps = seq // tm
    hb = tm // CONV_HALO
    wpad = -(-width // SUBLANES) * SUBLANES
    dww = jnp.pad(dw_w, ((0, wpad - width), (0, 0)))
    return pl.pallas_call(
        functools.partial(_conv_out_kernel, width=width, tiles_per_seq=tps),
        out_shape=jax.ShapeDtypeStruct((m, d), F32),
        grid=(m // tm, d // tn),
        in_specs=[
            pl.BlockSpec((tm, d), lambda i, j: (i, 0)),
            pl.BlockSpec((CONV_HALO, d), lambda i, j: (jnp.maximum(i * hb - 1, 0), 0)),
            pl.BlockSpec((wpad, d), lambda i, j: (0, 0)),
            pl.BlockSpec((1, d), lambda i, j: (0, 0)),
            pl.BlockSpec((1, d), lambda i, j: (0, 0)),
            pl.BlockSpec((1, d), lambda i, j: (0, 0)),
            pl.BlockSpec((d, tn), lambda i, j: (0, j)),
            pl.BlockSpec((1, tn), lambda i, j: (0, j)),
            pl.BlockSpec((tm, tn), lambda i, j: (i, j)),
            pl.BlockSpec((1, 1, tn), lambda i, j: ((i // tps) * 6 + 2, 0, j)),
        ],
        out_specs=pl.BlockSpec((tm, tn), lambda i, j: (i, j)),
        scratch_shapes=[
            pltpu.VMEM((tm + CONV_HALO, d), F32),
            pltpu.VMEM((tm, d), F32),
            pltpu.VMEM((tm, d), BF16),
        ],
        compiler_params=_params("parallel", "arbitrary"),
        name="conv_out",
    )(u, u, dww, dw_b.reshape(1, d), ln_g.reshape(1, d), ln_b.reshape(1, d), w_out, b_out.reshape(1, d), x, modv)


def _residual_matmul(h_ref, w_ref, x_ref, g_ref, o_ref):
    tn = o_ref.shape[1]
    sub = min(PROJ_SUB, tn)
    for c in range(tn // sub):
        sl = slice(c * sub, (c + 1) * sub)
        o_ref[:, sl] = x_ref[:, sl] + (1.0 + g_ref[0, :, sl]) * _dot(h_ref[...], w_ref[:, sl])


def _oproj_kernel(a_ref, w_ref, x_ref, g_ref, o_ref):
    _residual_matmul(a_ref, w_ref, x_ref, g_ref, o_ref)


def _oproj_gated_kernel(a_ref, gt_ref, w_ref, x_ref, g_ref, o_ref, h_scr):
    @pl.when(pl.program_id(1) == 0)
    def _():
        h_scr[...] = (a_ref[...].astype(F32) * _sigmoid(gt_ref[...].astype(F32))).astype(BF16)

    _residual_matmul(h_scr, w_ref, x_ref, g_ref, o_ref)


def _oproj(a, w, x, modv, seq, gate_src=None, gate_block=0):
    m, d = x.shape
    tm = _pick(seq, 1024, SUBLANES)
    tn = _pick(d, 1024, LANES)
    tps = seq // tm
    common = [
        pl.BlockSpec((d, tn), lambda i, j: (0, j)),
        pl.BlockSpec((tm, tn), lambda i, j: (i, j)),
        pl.BlockSpec((1, 1, tn), lambda i, j: ((i // tps) * 6 + 2, 0, j)),
    ]
    if gate_src is None:
        kern, in_specs, args, scratch = _oproj_kernel, [pl.BlockSpec((tm, d), lambda i, j: (i, 0))], [a], []
        name = "attn_oproj"
    else:
        kern = _oproj_gated_kernel
        in_specs = [pl.BlockSpec((tm, d), lambda i, j: (i, 0)),
                    pl.BlockSpec((tm, d), lambda i, j: (i, gate_block))]
        args = [a, gate_src]
        scratch = [pltpu.VMEM((tm, d), BF16)]
        name = "attn_oproj_gated"
    return pl.pallas_call(
        kern,
        out_shape=jax.ShapeDtypeStruct((m, d), F32),
        grid=(m // tm, d // tn),
        in_specs=in_specs + common,
        out_specs=pl.BlockSpec((tm, tn), lambda i, j: (i, j)),
        scratch_shapes=scratch,
        compiler_params=_params("parallel", "arbitrary"),
        name=name,
    )(*args, w, x, modv)


MLP_OC = 512


def _mlp_kernel(x_ref, ng_ref, sc_ref, sh_ref, g_ref, w1_ref, w2_ref, o_ref, h_scr):
    f = pl.program_id(1)
    d = o_ref.shape[1]

    @pl.when(f == 0)
    def _():
        _normmod(x_ref, ng_ref, sc_ref, sh_ref, h_scr)
        o_ref[...] = jnp.zeros_like(o_ref)

    u = _dot(h_scr[...], w1_ref[...])
    u = jnp.square(jnp.maximum(u, 0.0)).astype(BF16)
    oc = min(MLP_OC, d)
    for c in range(d // oc):
        sl = slice(c * oc, (c + 1) * oc)
        o_ref[:, sl] += _dot(u, w2_ref[:, sl])

    @pl.when(f == pl.num_programs(1) - 1)
    def _():
        o_ref[...] = x_ref[...] + (1.0 + g_ref[0]) * o_ref[...]


def _mlp(x, ng, modv, w1, w2, seq):
    m, d = x.shape
    dff = w1.shape[1]
    tm = _pick(seq, 1024, SUBLANES)
    tf = _pick(dff, 512, LANES)
    tps = seq // tm
    return pl.pallas_call(
        _mlp_kernel,
        out_shape=jax.ShapeDtypeStruct((m, d), F32),
        grid=(m // tm, dff // tf),
        in_specs=[
            pl.BlockSpec((tm, d), lambda i, f: (i, 0)),
            pl.BlockSpec((1, d), lambda i, f: (0, 0)),
            pl.BlockSpec((1, 1, d), lambda i, f: ((i // tps) * 6 + 4, 0, 0)),
            pl.BlockSpec((1, 1, d), lambda i, f: ((i // tps) * 6 + 3, 0, 0)),
            pl.BlockSpec((1, 1, d), lambda i, f: ((i // tps) * 6 + 5, 0, 0)),
            pl.BlockSpec((d, tf), lambda i, f: (0, f)),
            pl.BlockSpec((tf, d), lambda i, f: (f, 0)),
        ],
        out_specs=pl.BlockSpec((tm, d), lambda i, f: (i, 0)),
        scratch_shapes=[pltpu.VMEM((tm, d), BF16)],
        compiler_params=_params("parallel", "arbitrary"),
        name="mlp_fused",
    )(x, ng.reshape(1, d), modv, modv, modv, w1, w2)


def _cumsum_kernel(x_ref, u_ref, e_ref, m_ref, o_ref):
    hi = lax.Precision.HIGHEST
    x = x_ref[0]
    xc = jnp.dot(x, u_ref[...], precision=hi, preferred_element_type=F32)
    tot = jnp.dot(xc, e_ref[...], precision=hi, preferred_element_type=F32)
    o_ref[0] = xc + jnp.dot(m_ref[...], tot, precision=hi, preferred_element_type=F32)


def _cumsum_seq(lf):
    b, h, s = lf.shape
    nblk = s // LANES
    r = h * nblk
    x = lf.reshape(b, r, LANES)
    li = jnp.arange(LANES)
    upper = (li[:, None] <= li[None, :]).astype(F32)
    last = jnp.broadcast_to((li[:, None] == LANES - 1), (LANES, LANES)).astype(F32)
    ri = jnp.arange(r)
    prev = ((ri[:, None] // nblk == ri[None, :] // nblk) & (ri[None, :] < ri[:, None])).astype(F32)
    out = pl.pallas_call(
        _cumsum_kernel,
        out_shape=jax.ShapeDtypeStruct((b, r, LANES), F32),
        grid=(b,),
        in_specs=[
            pl.BlockSpec((1, r, LANES), lambda i: (i, 0, 0)),
            pl.BlockSpec((LANES, LANES), lambda i: (0, 0)),
            pl.BlockSpec((LANES, LANES), lambda i: (0, 0)),
            pl.BlockSpec((r, r), lambda i: (0, 0)),
        ],
        out_specs=pl.BlockSpec((1, r, LANES), lambda i: (i, 0, 0)),
        compiler_params=_params("parallel"),
        name="fox_cumsum",
    )(x, upper, last, prev)
    return out.reshape(b, h, s)


HEADS_PER_STEP = 2
KW = 2 * LANES
VT_ROWS = LANES + BF16_SUBLANES


def _split3(c):
    c1 = c.astype(BF16).astype(F32)
    r1 = c - c1
    c2 = r1.astype(BF16).astype(F32)
    c3 = (r1 - c2).astype(BF16).astype(F32)
    return c1, c2, c3


def _lane_column(tile, h):
    lane = lax.broadcasted_iota(jnp.int32, tile.shape, 1)
    return jnp.sum(jnp.where(lane == h, tile, 0.0), axis=-1, keepdims=True)


def _kv_prep_kernel(*refs, fox):
    if fox:
        k_ref, v_ref, ct_ref, ka_ref, vt_ref = refs
    else:
        k_ref, v_ref, ka_ref, vt_ref = refs
    hp = pl.program_id(1)
    i = pl.program_id(2)
    ts = k_ref.shape[1]
    lane = lax.broadcasted_iota(jnp.int32, (ts, LANES), 1)
    for hh in range(HEADS_PER_STEP):
        sl = slice(hh * LANES, (hh + 1) * LANES)
        if fox:
            ck = _lane_column(ct_ref[0], hp * HEADS_PER_STEP + hh) * LOG2E
            c1, c2, c3 = _split3(ck)
            extra = jnp.where(lane == 0, -c1, jnp.where(lane == 1, -c2, jnp.where(lane == 2, -c3,
                              jnp.where(lane < 6, 1.0, 0.0))))
        else:
            row = lax.broadcasted_iota(jnp.int32, (ts, LANES), 0) + i * ts
            extra = jnp.where(lane == lax.shift_right_logical(row, int(math.log2(MOBA_BLOCK))), 1.0, 0.0)
        ka_ref[0, hh, :, 0:LANES] = k_ref[0, :, sl]
        ka_ref[0, hh, :, LANES:KW] = extra.astype(BF16)
        sub = lax.broadcasted_iota(jnp.int32, (BF16_SUBLANES, ts), 0)
        vt_ref[0, hh, 0:LANES, :] = v_ref[0, :, sl].astype(F32).T.astype(BF16)
        vt_ref[0, hh, LANES:VT_ROWS, :] = jnp.where(sub == 0, 1.0, 0.0).astype(BF16)


def _kv_prep(karr, k_blk, varr, v_blk, cum_t, bsz, seq, nh):
    fox = cum_t is not None
    ts = _pick(seq, 1024, LANES)
    hps = HEADS_PER_STEP
    in_specs = [
        pl.BlockSpec((1, ts, hps * LANES), lambda b, h, i: (b, i, k_blk + h)),
        pl.BlockSpec((1, ts, hps * LANES), lambda b, h, i: (b, i, v_blk + h)),
    ]
    args = [karr, varr]
    if fox:
        in_specs.append(pl.BlockSpec((1, ts, LANES), lambda b, h, i: (b, i, 0)))
        args.append(cum_t)
    return pl.pallas_call(
        functools.partial(_kv_prep_kernel, fox=fox),
        out_shape=[jax.ShapeDtypeStruct((bsz, nh, seq, KW), BF16),
                   jax.ShapeDtypeStruct((bsz, nh, VT_ROWS, seq), BF16)],
        grid=(bsz, nh // hps, seq // ts),
        in_specs=in_specs,
        out_specs=[pl.BlockSpec((1, hps, ts, KW), lambda b, h, i: (b, h, i, 0)),
                   pl.BlockSpec((1, hps, VT_ROWS, ts), lambda b, h, i: (b, h, 0, i))],
        compiler_params=_params("parallel", "parallel", "parallel"),
        name="fox_kv_prep" if fox else "moba_kv_prep",
    )(*args)


def _attn_kernel(*refs, fox, tq, nb_pad, topk):
    if fox:
        q_ref, ka_ref, vt_ref, ct_ref, o_ref, qa_scr, m_scr, acc_scr, sa_scr, sb_scr = refs
    else:
        q_ref, ka_ref, vt_ref, ind_ref, o_ref, qa_scr, m_scr, acc_scr, sa_scr, sb_scr, km_scr = refs
    hp = pl.program_id(1)
    i = pl.program_id(2)
    hps = HEADS_PER_STEP

    for hh in range(hps):
        q = q_ref[0, :, hh * LANES:(hh + 1) * LANES]
        if fox:
            lane = lax.broadcasted_iota(jnp.int32, (tq, LANES), 1)
            cq = _lane_column(ct_ref[0], hp * hps + hh) * LOG2E
            c1, c2, c3 = _split3(cq)
            extra = jnp.where(lane < 3, 1.0, jnp.where(lane == 3, c1, jnp.where(lane == 4, c2,
                              jnp.where(lane == 5, c3, 0.0))))
        else:
            @pl.when(i == 0)
            def _():
                km_scr[hh] = (_dot(ind_ref[...], ka_ref[0, hh, :, 0:LANES]) * (1.0 / MOBA_BLOCK)).astype(BF16)

            gate = _dot_t(km_scr[hh, 0:nb_pad, :], q)
            blk = lax.broadcasted_iota(jnp.int32, (nb_pad, tq), 0)
            blk_f = blk.astype(F32)
            qpos = lax.broadcasted_iota(jnp.int32, (nb_pad, tq), 1) + i * tq
            cur = lax.shift_right_logical(qpos, int(math.log2(MOBA_BLOCK)))
            valid = blk < cur
            gate = jnp.where(valid, gate, NEG_INF)
            picked = jnp.zeros((nb_pad, tq), F32)
            for _ in range(topk):
                mx = jnp.max(gate, axis=0, keepdims=True)
                first = jnp.min(jnp.where(gate == mx, blk_f, float(nb_pad)), axis=0, keepdims=True)
                one = blk_f == first
                picked = jnp.where(one, 1.0, picked)
                gate = jnp.where(one, -jnp.inf, gate)
            allowed = ((picked > 0.0) & valid) | (blk == cur)
            sel_t = jnp.where(allowed, 0.0, NEG_INF)
            if nb_pad < LANES:
                sel_t = jnp.concatenate([sel_t, jnp.zeros((LANES - nb_pad, tq), F32)], axis=0)
            extra = sel_t.T
        qa_scr[hh, :, 0:LANES] = q
        qa_scr[hh, :, LANES:KW] = extra.astype(BF16)
        m_scr[hh] = jnp.full((SUBLANES, tq), NEG_INF, F32)
        acc_scr[hh] = jnp.zeros((VT_ROWS, tq), F32)

    def fill(s_buf, j):
        k0 = pl.multiple_of(j * tq, tq)
        for hh in range(hps):
            s_buf[hh] = _dot_t(ka_ref[0, hh, pl.ds(k0, tq), :], qa_scr[hh])

    def consume(s_buf, j, masked):
        k0 = pl.multiple_of(j * tq, tq)
        for hh in range(hps):
            vt = vt_ref[0, hh, :, pl.ds(k0, tq)]
            s = s_buf[hh]
            if masked:
                key = lax.broadcasted_iota(jnp.int32, (tq, tq), 0)
                qry = lax.broadcasted_iota(jnp.int32, (tq, tq), 1)
                s = jnp.where(key <= qry, s, NEG_INF)
            m_prev = m_scr[hh, 0:1, :]
            m_new = jnp.maximum(m_prev, jnp.max(s, axis=0, keepdims=True))
            alpha = jnp.exp2(m_prev - m_new)
            p = jnp.exp2(s - m_new).astype(BF16)
            acc_scr[hh] = alpha * acc_scr[hh] + _dot(vt, p)
            m_scr[hh] = jnp.broadcast_to(m_new, (SUBLANES, tq))

    fill(sa_scr, 0)
    npairs = i // 2

    def pair(t, carry):
        j0 = 2 * t
        fill(sb_scr, j0 + 1)
        consume(sa_scr, j0, False)
        fill(sa_scr, j0 + 2)
        consume(sb_scr, j0 + 1, False)
        return carry

    lax.fori_loop(0, npairs, pair, 0)

    @pl.when(i == 2 * npairs)
    def _():
        consume(sa_scr, i, True)

    @pl.when(i != 2 * npairs)
    def _():
        fill(sb_scr, i)
        consume(sa_scr, i - 1, False)
        consume(sb_scr, i, True)

    for hh in range(hps):
        acc = acc_scr[hh]
        o_t = acc[0:LANES, :] / acc[LANES:LANES + 1, :]
        o_ref[0, :, hh * LANES:(hh + 1) * LANES] = o_t.T.astype(BF16)


def _attention(arr, q_blk, ka, vt, cum_t, bsz, seq, nh):
    fox = cum_t is not None
    hps = HEADS_PER_STEP
    d = nh * LANES
    tq = _pick(seq, 512, LANES)
    nb = seq // MOBA_BLOCK
    nb_pad = -(-nb // SUBLANES) * SUBLANES
    in_specs = [
        pl.BlockSpec((1, tq, hps * LANES), lambda b, h, i: (b, i, q_blk + h)),
        pl.BlockSpec((1, hps, seq, KW), lambda b, h, i: (b, h, 0, 0)),
        pl.BlockSpec((1, hps, VT_ROWS, seq), lambda b, h, i: (b, h, 0, 0)),
    ]
    args = [arr, ka, vt]
    scratch = [pltpu.VMEM((hps, tq, KW), BF16), pltpu.VMEM((hps, SUBLANES, tq), F32),
               pltpu.VMEM((hps, VT_ROWS, tq), F32), pltpu.VMEM((hps, tq, tq), F32),
               pltpu.VMEM((hps, tq, tq), F32)]
    if fox:
        in_specs.append(pl.BlockSpec((1, tq, LANES), lambda b, h, i: (b, i, 0)))
        args.append(cum_t)
    else:
        assert seq % MOBA_BLOCK == 0 and nb_pad <= LANES and tq % MOBA_BLOCK == 0
        ind = (jnp.arange(LANES)[:, None] == (jnp.arange(seq) // MOBA_BLOCK)[None, :]).astype(BF16)
        in_specs.append(pl.BlockSpec((LANES, seq), lambda b, h, i: (0, 0)))
        args.append(ind)
        scratch.append(pltpu.VMEM((hps, LANES, LANES), BF16))
    out = pl.pallas_call(
        functools.partial(_attn_kernel, fox=fox, tq=tq, nb_pad=nb_pad, topk=MOBA_TOPK),
        out_shape=jax.ShapeDtypeStruct((bsz, seq, d), BF16),
        grid=(bsz, nh // hps, seq // tq),
        in_specs=in_specs,
        out_specs=pl.BlockSpec((1, tq, hps * LANES), lambda b, h, i: (b, i, h)),
        scratch_shapes=scratch,
        compiler_params=_params("parallel", "parallel", "arbitrary"),
        name="fox_attention" if fox else "moba_attention",
    )(*args)
    return out.reshape(bsz * seq, d)


def kernel(x, c, norm1_g, norm2_g, mod_w, mod_b, mlp_w1, mlp_w2, conv_w_in, conv_b_in, conv_dw_w, conv_dw_b,
           conv_ln_g, conv_ln_b, conv_w_out, conv_b_out, moba_w_qkv, moba_q_g, moba_k_g, moba_w_o, fox_w_in,
           fox_b_f, fox_q_g, fox_k_g, fox_w_o):
    bsz, seq, d = x.shape
    depth = mod_w.shape[0]
    hd = moba_q_g.shape[-1]
    nh = d // hd
    assert hd == LANES and nh <= LANES and nh % HEADS_PER_STEP == 0
    hps = HEADS_PER_STEP
    qscale = hd ** -0.5 * LOG2E

    mod = _mod_vectors(c, mod_w, mod_b)
    xf = x.reshape(bsz * seq, d)

    for i in range(depth):
        kind = i % N_MIXERS
        jx = i // N_MIXERS
        modv = mod[i].reshape(bsz * 6, 1, d)
        if kind == 0:
            u = _conv_in(xf, norm1_g[i], modv, conv_w_in[jx].astype(BF16), conv_b_in[jx], seq)
            xf = _conv_out(u, xf, modv, conv_dw_w[jx], conv_dw_b[jx], conv_ln_g[jx], conv_ln_b[jx],
                           conv_w_out[jx].astype(BF16), conv_b_out[jx], seq)
        elif kind == 1:
            w = moba_w_qkv[jx].astype(BF16)
            gvec = jnp.concatenate([jnp.tile(moba_q_g[jx], nh) * qscale, jnp.tile(moba_k_g[jx], nh)]).reshape(1, 2 * d)
            (qk,) = _heads_proj(xf, norm1_g[i], modv, w, 0, 2 * d, gvec, seq)
            (v,) = _heads_proj(xf, norm1_g[i], modv, w, 2 * d, d, None, seq)
            qk = qk.reshape(bsz, seq, 2 * d)
            ka, vt = _kv_prep(qk, nh // hps, v.reshape(bsz, seq, d), 0, None, bsz, seq, nh)
            o = _attention(qk, 0, ka, vt, None, bsz, seq, nh)
            xf = _oproj(o, moba_w_o[jx].astype(BF16), xf, modv, seq)
        else:
            w_in = fox_w_in[jx]
            w = w_in[:, :4 * d].astype(BF16)
            gvec = jnp.concatenate([jnp.tile(fox_q_g[jx], nh) * qscale, jnp.tile(fox_k_g[jx], nh)]).reshape(1, 2 * d)
            wf = jnp.pad(w_in[:, 4 * d:], ((0, 0), (0, LANES - nh))).astype(BF16)
            bf = jnp.pad(fox_b_f[jx], (0, LANES - nh)).reshape(1, LANES)
            qk, lf = _heads_proj(xf, norm1_g[i], modv, w, 0, 2 * d, gvec, seq, wf, bf)
            (vg,) = _heads_proj(xf, norm1_g[i], modv, w, 2 * d, 2 * d, None, seq)
            lf = lf[:, :nh].reshape(bsz, seq, nh).transpose(0, 2, 1)
            cum = _cumsum_seq(lf)
            cum_t = jnp.pad(cum.transpose(0, 2, 1), ((0, 0), (0, 0), (0, LANES - nh)))
            qk = qk.reshape(bsz, seq, 2 * d)
            ka, vt = _kv_prep(qk, nh // hps, vg.reshape(bsz, seq, 2 * d), 0, cum_t, bsz, seq, nh)
            o = _attention(qk, 0, ka, vt, cum_t, bsz, seq, nh)
            xf = _oproj(o, fox_w_o[jx].astype(BF16), xf, modv, seq, gate_src=vg, gate_block=1)
        xf = _mlp(xf, norm2_g[i], modv, mlp_w1[i].astype(BF16), mlp_w2[i].astype(BF16), seq)
    return xf.reshape(bsz, seq, d)
```

```python
import functools
import math

import jax
import jax.numpy as jnp
from jax import lax
from jax.experimental import pallas as pl
from jax.experimental.pallas import tpu as pltpu

N_MIXERS = 3
MOBA_BLOCK = 256
MOBA_TOPK = 3
NORM_EPS = 1e-6
NEG_INF = -1e30

LANES = 128
SUBLANES = 8
BF16_SUBLANES = 16
VMEM_LIMIT = 56 * 1024 * 1024

LOG2E = math.log2(math.e)

BF16 = jnp.bfloat16
F32 = jnp.float32


def _pick(n, pref, mult):
    if n <= pref:
        return n
    t = (pref // mult) * mult
    while t > mult and n % t:
        t -= mult
    assert n % t == 0, (n, pref, mult)
    return t


def _params(*sem, vmem_limit=VMEM_LIMIT):
    return pltpu.CompilerParams(dimension_semantics=sem, vmem_limit_bytes=vmem_limit)


def _dot(a, b):
    return jnp.dot(a, b, preferred_element_type=F32)


def _dot_t(a, b):
    return lax.dot_general(a, b, (((1,), (1,)), ((), ())), preferred_element_type=F32)


def _sigmoid(x):
    return 1.0 / (1.0 + jnp.exp(-x))


def _log_sigmoid(x):
    return -(jnp.maximum(-x, 0.0) + jnp.log1p(jnp.exp(-jnp.abs(x))))


NORM_ROWS = 64


def _normmod(x_ref, ng_ref, sc_ref, sh_ref, h_ref):
    tm = x_ref.shape[0]
    rows = min(NORM_ROWS, tm)
    ng = ng_ref[...]
    sc1 = 1.0 + sc_ref[0]
    sh = sh_ref[0]
    for c in range(tm // rows):
        sl = slice(c * rows, (c + 1) * rows)
        x = x_ref[sl, :]
        ms = jnp.mean(x * x, axis=-1, keepdims=True)
        y = x * lax.rsqrt(ms + NORM_EPS) * ng
        h_ref[sl, :] = (y * sc1 + sh).astype(h_ref.dtype)


def _mod_kernel(c_ref, w_ref, b_ref, o_ref):
    c = c_ref[...]
    cond = c * _sigmoid(c)
    o_ref[0] = _dot(cond, w_ref[0]) + b_ref[0]


def _mod_vectors(c, mod_w, mod_b):
    depth, d, n = mod_w.shape
    b = c.shape[0]
    bp = -(-b // SUBLANES) * SUBLANES
    cp = jnp.pad(c, ((0, bp - b), (0, 0)))
    tn = _pick(n, 1024, LANES)
    out = pl.pallas_call(
        _mod_kernel,
        out_shape=jax.ShapeDtypeStruct((depth, bp, n), F32),
        grid=(depth, n // tn),
        in_specs=[
            pl.BlockSpec((bp, d), lambda i, j: (0, 0)),
            pl.BlockSpec((1, d, tn), lambda i, j: (i, 0, j)),
            pl.BlockSpec((1, 1, tn), lambda i, j: (i, 0, j)),
        ],
        out_specs=pl.BlockSpec((1, bp, tn), lambda i, j: (i, 0, j)),
        compiler_params=_params("parallel", "parallel"),
        name="mod_vectors",
    )(cp, mod_w, mod_b.reshape(depth, 1, n))
    return out[:, :b]


PROJ_SUB = 512


def _glu_kernel(x_ref, ng_ref, sc_ref, sh_ref, wa_ref, wg_ref, ba_ref, bg_ref, o_ref, h_scr):
    @pl.when(pl.program_id(1) == 0)
    def _():
        _normmod(x_ref, ng_ref, sc_ref, sh_ref, h_scr)

    tn = o_ref.shape[1]
    sub = min(PROJ_SUB, tn)
    for c in range(tn // sub):
        sl = slice(c * sub, (c + 1) * sub)
        a = _dot(h_scr[...], wa_ref[:, sl]) + ba_ref[:, sl]
        g = _dot(h_scr[...], wg_ref[:, sl]) + bg_ref[:, sl]
        o_ref[:, sl] = a * _sigmoid(g)


def _conv_in(x, ng, modv, w_in, b_in, seq):
    m, d = x.shape
    tm = _pick(seq, 1024, SUBLANES)
    tn = _pick(d, 1024, LANES)
    nj = d // tn
    tps = seq // tm
    b2 = b_in.reshape(1, 2 * d)
    return pl.pallas_call(
        _glu_kernel,
        out_shape=jax.ShapeDtypeStruct((m, d), F32),
        grid=(m // tm, nj),
        in_specs=[
            pl.BlockSpec((tm, d), lambda i, j: (i, 0)),
            pl.BlockSpec((1, d), lambda i, j: (0, 0)),
            pl.BlockSpec((1, 1, d), lambda i, j: ((i // tps) * 6 + 1, 0, 0)),
            pl.BlockSpec((1, 1, d), lambda i, j: ((i // tps) * 6 + 0, 0, 0)),
            pl.BlockSpec((d, tn), lambda i, j: (0, j)),
            pl.BlockSpec((d, tn), lambda i, j: (0, j + nj)),
            pl.BlockSpec((1, tn), lambda i, j: (0, j)),
            pl.BlockSpec((1, tn), lambda i, j: (0, j + nj)),
        ],
        out_specs=pl.BlockSpec((tm, tn), lambda i, j: (i, j)),
        scratch_shapes=[pltpu.VMEM((tm, d), BF16)],
        compiler_params=_params("parallel", "arbitrary"),
        name="conv_in_glu",
    )(x, ng.reshape(1, d), modv, modv, w_in, w_in, b2, b2)


def _heads_kernel(x_ref, ng_ref, sc_ref, sh_ref, w_ref, gv_ref, *rest, with_gate):
    if with_gate:
        wf_ref, bf_ref, o_ref, h_ref, lf_ref = rest
    else:
        o_ref, h_ref = rest

    @pl.when(pl.program_id(1) == 0)
    def _():
        _normmod(x_ref, ng_ref, sc_ref, sh_ref, h_ref)
        if with_gate:
            lf_ref[...] = _log_sigmoid(_dot(h_ref[...], wf_ref[...]) + bf_ref[...])

    tn = o_ref.shape[1]
    sub = min(PROJ_SUB, tn)
    for c in range(tn // sub):
        r = _dot(h_ref[...], w_ref[:, c * sub:(c + 1) * sub])
        for hh in range(sub // LANES):
            sl = slice(c * sub + hh * LANES, c * sub + (hh + 1) * LANES)
            rh = r[:, hh * LANES:(hh + 1) * LANES]
            ms = jnp.mean(rh * rh, axis=-1, keepdims=True)
            o_ref[:, sl] = (rh * lax.rsqrt(ms + NORM_EPS) * gv_ref[:, sl]).astype(BF16)


def _heads_proj(x, ng, modv, w, ncols, gvec, seq, wf=None, bf=None):
    m, d = x.shape
    tm = _pick(seq, 1024, SUBLANES)
    tn = _pick(ncols, 1024, LANES)
    tps = seq // tm
    with_gate = wf is not None
    in_specs = [
        pl.BlockSpec((tm, d), lambda i, j: (i, 0)),
        pl.BlockSpec((1, d), lambda i, j: (0, 0)),
        pl.BlockSpec((1, 1, d), lambda i, j: ((i // tps) * 6 + 1, 0, 0)),
        pl.BlockSpec((1, 1, d), lambda i, j: ((i // tps) * 6 + 0, 0, 0)),
        pl.BlockSpec((d, tn), lambda i, j: (0, j)),
        pl.BlockSpec((1, tn), lambda i, j: (0, j)),
    ]
    args = [x, ng.reshape(1, d), modv, modv, w, gvec]
    out_shape = [jax.ShapeDtypeStruct((m, ncols), BF16), jax.ShapeDtypeStruct((m, d), BF16)]
    out_specs = [pl.BlockSpec((tm, tn), lambda i, j: (i, j)), pl.BlockSpec((tm, d), lambda i, j: (i, 0))]
    if with_gate:
        in_specs += [pl.BlockSpec((d, LANES), lambda i, j: (0, 0)), pl.BlockSpec((1, LANES), lambda i, j: (0, 0))]
        args += [wf, bf]
        out_shape.append(jax.ShapeDtypeStruct((m, LANES), F32))
        out_specs.append(pl.BlockSpec((tm, LANES), lambda i, j: (i, 0)))
    return pl.pallas_call(
        functools.partial(_heads_kernel, with_gate=with_gate),
        out_shape=out_shape,
        grid=(m // tm, ncols // tn),
        in_specs=in_specs,
        out_specs=out_specs,
        compiler_params=_params("parallel", "arbitrary"),
        name="heads_proj_norm" + ("_gate" if with_gate else ""),
    )(*args)


def _plain_kernel(h_ref, w_ref, o_ref):
    tn = o_ref.shape[1]
    sub = min(PROJ_SUB, tn)
    for c in range(tn // sub):
        sl = slice(c * sub, (c + 1) * sub)
        o_ref[:, sl] = _dot(h_ref[...], w_ref[:, sl]).astype(BF16)


def _plain_proj(h, w, col0, ncols, seq):
    m, d = h.shape
    tm = _pick(seq, 1024, SUBLANES)
    tn = _pick(ncols, 1024, LANES)
    assert col0 % tn == 0
    off = col0 // tn
    return pl.pallas_call(
        _plain_kernel,
        out_shape=jax.ShapeDtypeStruct((m, ncols), BF16),
        grid=(m // tm, ncols // tn),
        in_specs=[pl.BlockSpec((tm, d), lambda i, j: (i, 0)), pl.BlockSpec((d, tn), lambda i, j: (0, j + off))],
        out_specs=pl.BlockSpec((tm, tn), lambda i, j: (i, j)),
        compiler_params=_params("parallel", "arbitrary"),
        name="heads_proj_plain",
    )(h, w)


CONV_HALO = 32
CONV_RC = 128
CONV_CC = 128


def _conv_out_kernel(u_ref, uh_ref, dww_ref, dwb_ref, lg_ref, lb_ref, w_ref, bo_ref, x_ref, g_ref,
                     o_ref, ubuf, cbuf, h_scr, *, width, tiles_per_seq):
    i = pl.program_id(0)
    j = pl.program_id(1)
    tm, d = u_ref.shape

    @pl.when(j == 0)
    def _():
        first = (i % tiles_per_seq) == 0
        ubuf[0:CONV_HALO, :] = jnp.where(first, 0.0, uh_ref[...])
        ubuf[CONV_HALO:CONV_HALO + tm, :] = u_ref[...]
        rc = min(CONV_RC, tm)
        cc = min(CONV_CC, d)
        base = CONV_HALO - (width - 1)

        def row_body(r, carry):
            r0 = pl.multiple_of(r * rc, rc)

            def col_body(c, carry2):
                c0 = pl.multiple_of(c * cc, cc)
                acc = jnp.zeros((rc, cc), F32) + dwb_ref[:, pl.ds(c0, cc)]
                win = ubuf[pl.ds(r0, rc + CONV_HALO), pl.ds(c0, cc)]
                rows = rc + CONV_HALO
                for res in range(SUBLANES):
                    sh = win if res == 0 else pltpu.roll(win, rows - res, axis=0)
                    for t in range(width):
                        if (base + t) % SUBLANES == res:
                            a0 = base + t - res
                            acc = acc + sh[a0:a0 + rc, :] * dww_ref[t:t + 1, pl.ds(c0, cc)]
                cbuf[pl.ds(r0, rc), pl.ds(c0, cc)] = acc
                return carry2

            return lax.fori_loop(0, d // cc, col_body, carry)

        lax.fori_loop(0, tm // rc, row_body, 0)
        c = cbuf[...]
        mu = jnp.mean(c, axis=-1, keepdims=True)
        cen = c - mu
        var = jnp.mean(cen * cen, axis=-1, keepdims=True)
        y = cen * lax.rsqrt(var + NORM_EPS) * lg_ref[...] + lb_ref[...]
        h_scr[...] = (y * _sigmoid(y)).astype(BF16)

    y = _dot(h_scr[...], w_ref[...]) + bo_ref[...]
    o_ref[...] = x_ref[...] + (1.0 + g_ref[0]) * y


def _conv_out(u, x, modv, dw_w, dw_b, ln_g, ln_b, w_out, b_out, seq):
    m, d = u.shape
    width = dw_w.shape[0]
    assert width - 1 <= CONV_HALO
    tm = _pick(seq, 512, CONV_HALO)
    tn = _pick(d, 512, LANES)
    tps = seq // tm
    hb = tm // CONV_HALO
    wpad = -(-width // SUBLANES) * SUBLANES
    dww = jnp.pad(dw_w, ((0, wpad - width), (0, 0)))
    return pl.pallas_call(
        functools.partial(_conv_out_kernel, width=width, tiles_per_seq=tps),
        out_shape=jax.ShapeDtypeStruct((m, d), F32),
        grid=(m // tm, d // tn),
        in_specs=[
            pl.BlockSpec((tm, d), lambda i, j: (i, 0)),
            pl.BlockSpec((CONV_HALO, d), lambda i, j: (jnp.maximum(i * hb - 1, 0), 0)),
            pl.BlockSpec((wpad, d), lambda i, j: (0, 0)),
            pl.BlockSpec((1, d), lambda i, j: (0, 0)),
            pl.BlockSpec((1, d), lambda i, j: (0, 0)),
            pl.BlockSpec((1, d), lambda i, j: (0, 0)),
            pl.BlockSpec((d, tn), lambda i, j: (0, j)),
            pl.BlockSpec((1, tn), lambda i, j: (0, j)),
            pl.BlockSpec((tm, tn), lambda i, j: (i, j)),
            pl.BlockSpec((1, 1, tn), lambda i, j: ((i // tps) * 6 + 2, 0, j)),
        ],
        out_specs=pl.BlockSpec((tm, tn), lambda i, j: (i, j)),
        scratch_shapes=[
            pltpu.VMEM((tm + CONV_HALO, d), F32),
            pltpu.VMEM((tm, d), F32),
            pltpu.VMEM((tm, d), BF16),
        ],
        compiler_params=_params("parallel", "arbitrary"),
        name="conv_out",
    )(u, u, dww, dw_b.reshape(1, d), ln_g.reshape(1, d), ln_b.reshape(1, d), w_out, b_out.reshape(1, d), x, modv)


def _oproj_kernel(a_ref, w_ref, x_ref, g_ref, o_ref):
    tn = o_ref.shape[1]
    sub = min(PROJ_SUB, tn)
    for c in range(tn // sub):
        sl = slice(c * sub, (c + 1) * sub)
        o_ref[:, sl] = x_ref[:, sl] + (1.0 + g_ref[0, :, sl]) * _dot(a_ref[...], w_ref[:, sl])


def _oproj(a, w, x, modv, seq):
    m, d = x.shape
    tm = _pick(seq, 1024, SUBLANES)
    tn = _pick(d, 1024, LANES)
    tps = seq // tm
    return pl.pallas_call(
        _oproj_kernel,
        out_shape=jax.ShapeDtypeStruct((m, d), F32),
        grid=(m // tm, d // tn),
        in_specs=[
            pl.BlockSpec((tm, d), lambda i, j: (i, 0)),
            pl.BlockSpec((d, tn), lambda i, j: (0, j)),
            pl.BlockSpec((tm, tn), lambda i, j: (i, j)),
            pl.BlockSpec((1, 1, tn), lambda i, j: ((i // tps) * 6 + 2, 0, j)),
        ],
        out_specs=pl.BlockSpec((tm, tn), lambda i, j: (i, j)),
        compiler_params=_params("parallel", "arbitrary"),
        name="attn_oproj",
    )(a, w, x, modv)


MLP_OC = 512
MLP_VMEM_LIMIT = 58 * 1024 * 1024


def _mlp_kernel(x_ref, ng_ref, sc_ref, sh_ref, g_ref, w1_ref, w2_ref, o_ref, h_scr):
    f = pl.program_id(1)
    d = o_ref.shape[1]

    @pl.when(f == 0)
    def _():
        _normmod(x_ref, ng_ref, sc_ref, sh_ref, h_scr)
        o_ref[...] = jnp.zeros_like(o_ref)

    tf = w1_ref.shape[1]
    half = tf // 2
    us = []
    for p in range(2):
        r = _dot(h_scr[...], w1_ref[:, p * half:(p + 1) * half])
        us.append(jnp.square(jnp.maximum(r, 0.0)).astype(BF16))
    oc = min(MLP_OC, d)
    for c in range(d // oc):
        sl = slice(c * oc, (c + 1) * oc)
        o_ref[:, sl] += _dot(us[0], w2_ref[0:half, sl]) + _dot(us[1], w2_ref[half:tf, sl])

    @pl.when(f == pl.num_programs(1) - 1)
    def _():
        o_ref[...] = x_ref[...] + (1.0 + g_ref[0]) * o_ref[...]


def _mlp(x, ng, modv, w1, w2, seq):
    m, d = x.shape
    dff = w1.shape[1]
    tm = _pick(seq, 1024, SUBLANES)
    tf = _pick(dff, 1024, LANES)
    tps = seq // tm
    return pl.pallas_call(
        _mlp_kernel,
        out_shape=jax.ShapeDtypeStruct((m, d), F32),
        grid=(m // tm, dff // tf),
        in_specs=[
            pl.BlockSpec((tm, d), lambda i, f: (i, 0)),
            pl.BlockSpec((1, d), lambda i, f: (0, 0)),
            pl.BlockSpec((1, 1, d), lambda i, f: ((i // tps) * 6 + 4, 0, 0)),
            pl.BlockSpec((1, 1, d), lambda i, f: ((i // tps) * 6 + 3, 0, 0)),
            pl.BlockSpec((1, 1, d), lambda i, f: ((i // tps) * 6 + 5, 0, 0)),
            pl.BlockSpec((d, tf), lambda i, f: (0, f)),
            pl.BlockSpec((tf, d), lambda i, f: (f, 0)),
        ],
        out_specs=pl.BlockSpec((tm, d), lambda i, f: (i, 0)),
        scratch_shapes=[pltpu.VMEM((tm, d), BF16)],
        compiler_params=_params("parallel", "arbitrary", vmem_limit=MLP_VMEM_LIMIT),
        name="mlp_fused",
    )(x, ng.reshape(1, d), modv, modv, modv, w1, w2)


def _cumsum_kernel(x_ref, u_ref, e_ref, m_ref, o_ref):
    hi = lax.Precision.HIGHEST
    x = x_ref[0]
    xc = jnp.dot(x, u_ref[...], precision=hi, preferred_element_type=F32)
    tot = jnp.dot(xc, e_ref[...], precision=hi, preferred_element_type=F32)
    o_ref[0] = xc + jnp.dot(m_ref[...], tot, precision=hi, preferred_element_type=F32)


def _cumsum_seq(lf):
    b, h, s = lf.shape
    nblk = s // LANES
    r = h * nblk
    x = lf.reshape(b, r, LANES)
    li = jnp.arange(LANES)
    upper = (li[:, None] <= li[None, :]).astype(F32)
    last = jnp.broadcast_to((li[:, None] == LANES - 1), (LANES, LANES)).astype(F32)
    ri = jnp.arange(r)
    prev = ((ri[:, None] // nblk == ri[None, :] // nblk) & (ri[None, :] < ri[:, None])).astype(F32)
    out = pl.pallas_call(
        _cumsum_kernel,
        out_shape=jax.ShapeDtypeStruct((b, r, LANES), F32),
        grid=(b,),
        in_specs=[
            pl.BlockSpec((1, r, LANES), lambda i: (i, 0, 0)),
            pl.BlockSpec((LANES, LANES), lambda i: (0, 0)),
            pl.BlockSpec((LANES, LANES), lambda i: (0, 0)),
            pl.BlockSpec((r, r), lambda i: (0, 0)),
        ],
        out_specs=pl.BlockSpec((1, r, LANES), lambda i: (i, 0, 0)),
        compiler_params=_params("parallel"),
        name="fox_cumsum",
    )(x, upper, last, prev)
    return out.reshape(b, h, s)


HEADS_PER_STEP = 2
KW = 2 * LANES
VT_ROWS = LANES + BF16_SUBLANES


def _split3(c):
    c1 = c.astype(BF16).astype(F32)
    r1 = c - c1
    c2 = r1.astype(BF16).astype(F32)
    c3 = (r1 - c2).astype(BF16).astype(F32)
    return c1, c2, c3


def _lane_column(tile, h):
    lane = lax.broadcasted_iota(jnp.int32, tile.shape, 1)
    return jnp.sum(jnp.where(lane == h, tile, 0.0), axis=-1, keepdims=True)


def _kv_prep_kernel(*refs, fox):
    if fox:
        k_ref, v_ref, ct_ref, ka_ref, vt_ref = refs
    else:
        k_ref, v_ref, ka_ref, vt_ref = refs
    hp = pl.program_id(1)
    i = pl.program_id(2)
    ts = k_ref.shape[1]
    lane = lax.broadcasted_iota(jnp.int32, (ts, LANES), 1)
    for hh in range(HEADS_PER_STEP):
        sl = slice(hh * LANES, (hh + 1) * LANES)
        if fox:
            ck = _lane_column(ct_ref[0], hp * HEADS_PER_STEP + hh) * LOG2E
            c1, c2, c3 = _split3(ck)
            extra = jnp.where(lane == 0, -c1, jnp.where(lane == 1, -c2, jnp.where(lane == 2, -c3,
                              jnp.where(lane < 6, 1.0, 0.0))))
        else:
            row = lax.broadcasted_iota(jnp.int32, (ts, LANES), 0) + i * ts
            extra = jnp.where(lane == lax.shift_right_logical(row, int(math.log2(MOBA_BLOCK))), 1.0, 0.0)
        ka_ref[0, hh, :, 0:LANES] = k_ref[0, :, sl]
        ka_ref[0, hh, :, LANES:KW] = extra.astype(BF16)
        sub = lax.broadcasted_iota(jnp.int32, (BF16_SUBLANES, ts), 0)
        vt_ref[0, hh, 0:LANES, :] = v_ref[0, :, sl].astype(F32).T.astype(BF16)
        vt_ref[0, hh, LANES:VT_ROWS, :] = jnp.where(sub == 0, 1.0, 0.0).astype(BF16)


def _kv_prep(karr, k_blk, varr, v_blk, cum_t, bsz, seq, nh):
    fox = cum_t is not None
    ts = _pick(seq, 1024, LANES)
    hps = HEADS_PER_STEP
    in_specs = [
        pl.BlockSpec((1, ts, hps * LANES), lambda b, h, i: (b, i, k_blk + h)),
        pl.BlockSpec((1, ts, hps * LANES), lambda b, h, i: (b, i, v_blk + h)),
    ]
    args = [karr, varr]
    if fox:
        in_specs.append(pl.BlockSpec((1, ts, LANES), lambda b, h, i: (b, i, 0)))
        args.append(cum_t)
    return pl.pallas_call(
        functools.partial(_kv_prep_kernel, fox=fox),
        out_shape=[jax.ShapeDtypeStruct((bsz, nh, seq, KW), BF16),
                   jax.ShapeDtypeStruct((bsz, nh, VT_ROWS, seq), BF16)],
        grid=(bsz, nh // hps, seq // ts),
        in_specs=in_specs,
        out_specs=[pl.BlockSpec((1, hps, ts, KW), lambda b, h, i: (b, h, i, 0)),
                   pl.BlockSpec((1, hps, VT_ROWS, ts), lambda b, h, i: (b, h, 0, i))],
        compiler_params=_params("parallel", "parallel", "parallel"),
        name="fox_kv_prep" if fox else "moba_kv_prep",
    )(*args)


def _attn_kernel(*refs, fox, tq, nb_pad, topk):
    if fox:
        q_ref, ka_ref, vt_ref, ct_ref, og_ref, o_ref, qa_scr, m_scr, acc_scr, sa_scr, sb_scr = refs
    else:
        q_ref, ka_ref, vt_ref, ind_ref, o_ref, qa_scr, m_scr, acc_scr, sa_scr, sb_scr, km_scr = refs
    hp = pl.program_id(1)
    i = pl.program_id(2)
    hps = HEADS_PER_STEP

    for hh in range(hps):
        q = q_ref[0, :, hh * LANES:(hh + 1) * LANES]
        if fox:
            lane = lax.broadcasted_iota(jnp.int32, (tq, LANES), 1)
            cq = _lane_column(ct_ref[0], hp * hps + hh) * LOG2E
            c1, c2, c3 = _split3(cq)
            extra = jnp.where(lane < 3, 1.0, jnp.where(lane == 3, c1, jnp.where(lane == 4, c2,
                              jnp.where(lane == 5, c3, 0.0))))
        else:
            @pl.when(i == 0)
            def _():
                km_scr[hh] = (_dot(ind_ref[...], ka_ref[0, hh, :, 0:LANES]) * (1.0 / MOBA_BLOCK)).astype(BF16)

            gate = _dot_t(km_scr[hh, 0:nb_pad, :], q)
            blk = lax.broadcasted_iota(jnp.int32, (nb_pad, tq), 0)
            blk_f = blk.astype(F32)
            qpos = lax.broadcasted_iota(jnp.int32, (nb_pad, tq), 1) + i * tq
            cur = lax.shift_right_logical(qpos, int(math.log2(MOBA_BLOCK)))
            valid = blk < cur
            gate = jnp.where(valid, gate, NEG_INF)
            picked = jnp.zeros((nb_pad, tq), F32)
            for _ in range(topk):
                mx = jnp.max(gate, axis=0, keepdims=True)
                first = jnp.min(jnp.where(gate == mx, blk_f, float(nb_pad)), axis=0, keepdims=True)
                one = blk_f == first
                picked = jnp.where(one, 1.0, picked)
                gate = jnp.where(one, -jnp.inf, gate)
            allowed = ((picked > 0.0) & valid) | (blk == cur)
            sel_t = jnp.where(allowed, 0.0, NEG_INF)
            if nb_pad < LANES:
                sel_t = jnp.concatenate([sel_t, jnp.zeros((LANES - nb_pad, tq), F32)], axis=0)
            extra = sel_t.T
        qa_scr[hh, :, 0:LANES] = q
        qa_scr[hh, :, LANES:KW] = extra.astype(BF16)
        m_scr[hh] = jnp.full((SUBLANES, tq), NEG_INF, F32)
        acc_scr[hh] = jnp.zeros((VT_ROWS, tq), F32)

    def fill(s_buf, j):
        k0 = pl.multiple_of(j * tq, tq)
        for hh in range(hps):
            s_buf[hh] = _dot_t(ka_ref[0, hh, pl.ds(k0, tq), :], qa_scr[hh])

    def consume(s_buf, j, masked):
        k0 = pl.multiple_of(j * tq, tq)
        for hh in range(hps):
            vt = vt_ref[0, hh, :, pl.ds(k0, tq)]
            s = s_buf[hh]
            if masked:
                key = lax.broadcasted_iota(jnp.int32, (tq, tq), 0)
                qry = lax.broadcasted_iota(jnp.int32, (tq, tq), 1)
                s = jnp.where(key <= qry, s, NEG_INF)
            m_prev = m_scr[hh, 0:1, :]
            m_new = jnp.maximum(m_prev, jnp.max(s, axis=0, keepdims=True))
            alpha = jnp.exp2(m_prev - m_new)
            p = jnp.exp2(s - m_new).astype(BF16)
            acc_scr[hh] = alpha * acc_scr[hh] + _dot(vt, p)
            m_scr[hh] = jnp.broadcast_to(m_new, (SUBLANES, tq))

    fill(sa_scr, 0)
    npairs = i // 2

    def pair(t, carry):
        j0 = 2 * t
        fill(sb_scr, j0 + 1)
        consume(sa_scr, j0, False)
        fill(sa_scr, j0 + 2)
        consume(sb_scr, j0 + 1, False)
        return carry

    lax.fori_loop(0, npairs, pair, 0)

    @pl.when(i == 2 * npairs)
    def _():
        consume(sa_scr, i, True)

    @pl.when(i != 2 * npairs)
    def _():
        fill(sb_scr, i)
        consume(sa_scr, i - 1, False)
        consume(sb_scr, i, True)

    for hh in range(hps):
        acc = acc_scr[hh]
        o_t = acc[0:LANES, :] / acc[LANES:LANES + 1, :]
        out = o_t.T
        if fox:
            out = out * _sigmoid(og_ref[0, :, hh * LANES:(hh + 1) * LANES].astype(F32))
        o_ref[0, :, hh * LANES:(hh + 1) * LANES] = out.astype(BF16)


def _attention(arr, q_blk, ka, vt, cum_t, bsz, seq, nh, gate_arr=None, gate_blk=0):
    fox = cum_t is not None
    hps = HEADS_PER_STEP
    d = nh * LANES
    tq = _pick(seq, 512, LANES)
    nb = seq // MOBA_BLOCK
    nb_pad = -(-nb // SUBLANES) * SUBLANES
    in_specs = [
        pl.BlockSpec((1, tq, hps * LANES), lambda b, h, i: (b, i, q_blk + h)),
        pl.BlockSpec((1, hps, seq, KW), lambda b, h, i: (b, h, 0, 0)),
        pl.BlockSpec((1, hps, VT_ROWS, seq), lambda b, h, i: (b, h, 0, 0)),
    ]
    args = [arr, ka, vt]
    scratch = [pltpu.VMEM((hps, tq, KW), BF16), pltpu.VMEM((hps, SUBLANES, tq), F32),
               pltpu.VMEM((hps, VT_ROWS, tq), F32), pltpu.VMEM((hps, tq, tq), F32),
               pltpu.VMEM((hps, tq, tq), F32)]
    if fox:
        in_specs.append(pl.BlockSpec((1, tq, LANES), lambda b, h, i: (b, i, 0)))
        in_specs.append(pl.BlockSpec((1, tq, hps * LANES), lambda b, h, i: (b, i, gate_blk + h)))
        args += [cum_t, gate_arr]
    else:
        assert seq % MOBA_BLOCK == 0 and nb_pad <= LANES and tq % MOBA_BLOCK == 0
        ind = (jnp.arange(LANES)[:, None] == (jnp.arange(seq) // MOBA_BLOCK)[None, :]).astype(BF16)
        in_specs.append(pl.BlockSpec((LANES, seq), lambda b, h, i: (0, 0)))
        args.append(ind)
        scratch.append(pltpu.VMEM((hps, LANES, LANES), BF16))
    out = pl.pallas_call(
        functools.partial(_attn_kernel, fox=fox, tq=tq, nb_pad=nb_pad, topk=MOBA_TOPK),
        out_shape=jax.ShapeDtypeStruct((bsz, seq, d), BF16),
        grid=(bsz, nh // hps, seq // tq),
        in_specs=in_specs,
        out_specs=pl.BlockSpec((1, tq, hps * LANES), lambda b, h, i: (b, i, h)),
        scratch_shapes=scratch,
        compiler_params=_params("parallel", "parallel", "arbitrary"),
        name="fox_attention" if fox else "moba_attention",
    )(*args)
    return out.reshape(bsz * seq, d)


def kernel(x, c, norm1_g, norm2_g, mod_w, mod_b, mlp_w1, mlp_w2, conv_w_in, conv_b_in, conv_dw_w, conv_dw_b,
           conv_ln_g, conv_ln_b, conv_w_out, conv_b_out, moba_w_qkv, moba_q_g, moba_k_g, moba_w_o, fox_w_in,
           fox_b_f, fox_q_g, fox_k_g, fox_w_o):
    bsz, seq, d = x.shape
    depth = mod_w.shape[0]
    hd = moba_q_g.shape[-1]
    nh = d // hd
    assert hd == LANES and nh <= LANES and nh % HEADS_PER_STEP == 0
    hps = HEADS_PER_STEP
    qscale = hd ** -0.5 * LOG2E

    mod = _mod_vectors(c, mod_w, mod_b)
    xf = x.reshape(bsz * seq, d)

    for i in range(depth):
        kind = i % N_MIXERS
        jx = i // N_MIXERS
        modv = mod[i].reshape(bsz * 6, 1, d)
        if kind == 0:
            u = _conv_in(xf, norm1_g[i], modv, conv_w_in[jx].astype(BF16), conv_b_in[jx], seq)
            xf = _conv_out(u, xf, modv, conv_dw_w[jx], conv_dw_b[jx], conv_ln_g[jx], conv_ln_b[jx],
                           conv_w_out[jx].astype(BF16), conv_b_out[jx], seq)
        elif kind == 1:
            w = moba_w_qkv[jx].astype(BF16)
            gvec = jnp.concatenate([jnp.tile(moba_q_g[jx], nh) * qscale, jnp.tile(moba_k_g[jx], nh)]).reshape(1, 2 * d)
            qk, h = _heads_proj(xf, norm1_g[i], modv, w, 2 * d, gvec, seq)
            v = _plain_proj(h, w, 2 * d, d, seq)
            qk = qk.reshape(bsz, seq, 2 * d)
            ka, vt = _kv_prep(qk, nh // hps, v.reshape(bsz, seq, d), 0, None, bsz, seq, nh)
            o = _attention(qk, 0, ka, vt, None, bsz, seq, nh)
            xf = _oproj(o, moba_w_o[jx].astype(BF16), xf, modv, seq)
        else:
            w_in = fox_w_in[jx]
            w = w_in[:, :4 * d].astype(BF16)
            gvec = jnp.concatenate([jnp.tile(fox_q_g[jx], nh) * qscale, jnp.tile(fox_k_g[jx], nh)]).reshape(1, 2 * d)
            wf = jnp.pad(w_in[:, 4 * d:], ((0, 0), (0, LANES - nh))).astype(BF16)
            bf = jnp.pad(fox_b_f[jx], (0, LANES - nh)).reshape(1, LANES)
            qk, h, lf = _heads_proj(xf, norm1_g[i], modv, w, 2 * d, gvec, seq, wf, bf)
            vg = _plain_proj(h, w, 2 * d, 2 * d, seq)
            lf = lf[:, :nh].reshape(bsz, seq, nh).transpose(0, 2, 1)
            cum = _cumsum_seq(lf)
            cum_t = jnp.pad(cum.transpose(0, 2, 1), ((0, 0), (0, 0), (0, LANES - nh)))
            qk = qk.reshape(bsz, seq, 2 * d)
            ka, vt = _kv_prep(qk, nh // hps, vg.reshape(bsz, seq, 2 * d), 0, cum_t, bsz, seq, nh)
            vg = vg.reshape(bsz, seq, 2 * d)
            o = _attention(qk, 0, ka, vt, cum_t, bsz, seq, nh, gate_arr=vg, gate_blk=nh // hps)
            xf = _oproj(o, fox_w_o[jx].astype(BF16), xf, modv, seq)
        xf = _mlp(xf, norm2_g[i], modv, mlp_w1[i].astype(BF16), mlp_w2[i].astype(BF16), seq)
    return xf.reshape(bsz, seq, d)
```

```python
import functools
import math

import jax
import jax.numpy as jnp
from jax import lax
from jax.experimental import pallas as pl
from jax.experimental.pallas import tpu as pltpu

N_MIXERS = 3
MOBA_BLOCK = 256
MOBA_TOPK = 3
NORM_EPS = 1e-6
NEG_INF = -1e30

LANES = 128
SUBLANES = 8
BF16_SUBLANES = 16
VMEM_LIMIT = 56 * 1024 * 1024

LOG2E = math.log2(math.e)

BF16 = jnp.bfloat16
F32 = jnp.float32


def _pick(n, pref, mult):
    if n <= pref:
        return n
    t = (pref // mult) * mult
    while t > mult and n % t:
        t -= mult
    assert n % t == 0, (n, pref, mult)
    return t


def _params(*sem, vmem_limit=VMEM_LIMIT):
    return pltpu.CompilerParams(dimension_semantics=sem, vmem_limit_bytes=vmem_limit)


def _dot(a, b):
    return jnp.dot(a, b, preferred_element_type=F32)


def _dot_t(a, b):
    return lax.dot_general(a, b, (((1,), (1,)), ((), ())), preferred_element_type=F32)


def _sigmoid(x):
    return 1.0 / (1.0 + jnp.exp(-x))


def _log_sigmoid(x):
    return -(jnp.maximum(-x, 0.0) + jnp.log1p(jnp.exp(-jnp.abs(x))))


NORM_ROWS = 64


def _normmod(x_ref, ng_ref, sc_ref, sh_ref, h_ref):
    tm = x_ref.shape[0]
    rows = min(NORM_ROWS, tm)
    ng = ng_ref[...]
    sc1 = 1.0 + sc_ref[0]
    sh = sh_ref[0]
    for c in range(tm // rows):
        sl = slice(c * rows, (c + 1) * rows)
        x = x_ref[sl, :]
        ms = jnp.mean(x * x, axis=-1, keepdims=True)
        y = x * lax.rsqrt(ms + NORM_EPS) * ng
        h_ref[sl, :] = (y * sc1 + sh).astype(h_ref.dtype)


def _mod_kernel(c_ref, w_ref, b_ref, o_ref):
    c = c_ref[...]
    cond = c * _sigmoid(c)
    o_ref[0] = _dot(cond, w_ref[0]) + b_ref[0]


def _mod_vectors(c, mod_w, mod_b):
    depth, d, n = mod_w.shape
    b = c.shape[0]
    bp = -(-b // SUBLANES) * SUBLANES
    cp = jnp.pad(c, ((0, bp - b), (0, 0)))
    tn = _pick(n, 1024, LANES)
    out = pl.pallas_call(
        _mod_kernel,
        out_shape=jax.ShapeDtypeStruct((depth, bp, n), F32),
        grid=(depth, n // tn),
        in_specs=[
            pl.BlockSpec((bp, d), lambda i, j: (0, 0)),
            pl.BlockSpec((1, d, tn), lambda i, j: (i, 0, j)),
            pl.BlockSpec((1, 1, tn), lambda i, j: (i, 0, j)),
        ],
        out_specs=pl.BlockSpec((1, bp, tn), lambda i, j: (i, 0, j)),
        compiler_params=_params("parallel", "parallel"),
        name="mod_vectors",
    )(cp, mod_w, mod_b.reshape(depth, 1, n))
    return out[:, :b]


PROJ_SUB = 512


def _glu_kernel(x_ref, ng_ref, sc_ref, sh_ref, wa_ref, wg_ref, ba_ref, bg_ref, o_ref, h_scr):
    @pl.when(pl.program_id(1) == 0)
    def _():
        _normmod(x_ref, ng_ref, sc_ref, sh_ref, h_scr)

    tn = o_ref.shape[1]
    sub = min(PROJ_SUB, tn)
    for c in range(tn // sub):
        sl = slice(c * sub, (c + 1) * sub)
        a = _dot(h_scr[...], wa_ref[:, sl]) + ba_ref[:, sl]
        g = _dot(h_scr[...], wg_ref[:, sl]) + bg_ref[:, sl]
        o_ref[:, sl] = a * _sigmoid(g)


def _conv_in(x, ng, modv, w_in, b_in, seq):
    m, d = x.shape
    tm = _pick(seq, 1024, SUBLANES)
    tn = _pick(d, 1024, LANES)
    nj = d // tn
    tps = seq // tm
    b2 = b_in.reshape(1, 2 * d)
    return pl.pallas_call(
        _glu_kernel,
        out_shape=jax.ShapeDtypeStruct((m, d), F32),
        grid=(m // tm, nj),
        in_specs=[
            pl.BlockSpec((tm, d), lambda i, j: (i, 0)),
            pl.BlockSpec((1, d), lambda i, j: (0, 0)),
            pl.BlockSpec((1, 1, d), lambda i, j: ((i // tps) * 6 + 1, 0, 0)),
            pl.BlockSpec((1, 1, d), lambda i, j: ((i // tps) * 6 + 0, 0, 0)),
            pl.BlockSpec((d, tn), lambda i, j: (0, j)),
            pl.BlockSpec((d, tn), lambda i, j: (0, j + nj)),
            pl.BlockSpec((1, tn), lambda i, j: (0, j)),
            pl.BlockSpec((1, tn), lambda i, j: (0, j + nj)),
        ],
        out_specs=pl.BlockSpec((tm, tn), lambda i, j: (i, j)),
        scratch_shapes=[pltpu.VMEM((tm, d), BF16)],
        compiler_params=_params("parallel", "arbitrary"),
        name="conv_in_glu",
    )(x, ng.reshape(1, d), modv, modv, w_in, w_in, b2, b2)


def _heads_kernel(x_ref, ng_ref, sc_ref, sh_ref, w_ref, gv_ref, *rest, with_gate):
    if with_gate:
        wf_ref, bf_ref, o_ref, h_ref, lf_ref = rest
    else:
        o_ref, h_ref = rest

    @pl.when(pl.program_id(1) == 0)
    def _():
        _normmod(x_ref, ng_ref, sc_ref, sh_ref, h_ref)
        if with_gate:
            lf_ref[...] = _log_sigmoid(_dot(h_ref[...], wf_ref[...]) + bf_ref[...])

    tn = o_ref.shape[1]
    sub = min(PROJ_SUB, tn)
    for c in range(tn // sub):
        r = _dot(h_ref[...], w_ref[:, c * sub:(c + 1) * sub])
        for hh in range(sub // LANES):
            sl = slice(c * sub + hh * LANES, c * sub + (hh + 1) * LANES)
            rh = r[:, hh * LANES:(hh + 1) * LANES]
            ms = jnp.mean(rh * rh, axis=-1, keepdims=True)
            o_ref[:, sl] = (rh * lax.rsqrt(ms + NORM_EPS) * gv_ref[:, sl]).astype(BF16)


def _heads_proj(x, ng, modv, w, ncols, gvec, seq, wf=None, bf=None):
    m, d = x.shape
    tm = _pick(seq, 1024, SUBLANES)
    tn = _pick(ncols, 1024, LANES)
    tps = seq // tm
    with_gate = wf is not None
    in_specs = [
        pl.BlockSpec((tm, d), lambda i, j: (i, 0)),
        pl.BlockSpec((1, d), lambda i, j: (0, 0)),
        pl.BlockSpec((1, 1, d), lambda i, j: ((i // tps) * 6 + 1, 0, 0)),
        pl.BlockSpec((1, 1, d), lambda i, j: ((i // tps) * 6 + 0, 0, 0)),
        pl.BlockSpec((d, tn), lambda i, j: (0, j)),
        pl.BlockSpec((1, tn), lambda i, j: (0, j)),
    ]
    args = [x, ng.reshape(1, d), modv, modv, w, gvec]
    out_shape = [jax.ShapeDtypeStruct((m, ncols), BF16), jax.ShapeDtypeStruct((m, d), BF16)]
    out_specs = [pl.BlockSpec((tm, tn), lambda i, j: (i, j)), pl.BlockSpec((tm, d), lambda i, j: (i, 0))]
    if with_gate:
        in_specs += [pl.BlockSpec((d, LANES), lambda i, j: (0, 0)), pl.BlockSpec((1, LANES), lambda i, j: (0, 0))]
        args += [wf, bf]
        out_shape.append(jax.ShapeDtypeStruct((m, LANES), F32))
        out_specs.append(pl.BlockSpec((tm, LANES), lambda i, j: (i, 0)))
    return pl.pallas_call(
        functools.partial(_heads_kernel, with_gate=with_gate),
        out_shape=out_shape,
        grid=(m // tm, ncols // tn),
        in_specs=in_specs,
        out_specs=out_specs,
        compiler_params=_params("parallel", "arbitrary"),
        name="heads_proj_norm" + ("_gate" if with_gate else ""),
    )(*args)


def _plain_kernel(h_ref, w_ref, o_ref):
    tn = o_ref.shape[1]
    sub = min(PROJ_SUB, tn)
    for c in range(tn // sub):
        sl = slice(c * sub, (c + 1) * sub)
        o_ref[:, sl] = _dot(h_ref[...], w_ref[:, sl]).astype(BF16)


def _plain_proj(h, w, col0, ncols, seq):
    m, d = h.shape
    tm = _pick(seq, 1024, SUBLANES)
    tn = _pick(ncols, 1024, LANES)
    assert col0 % tn == 0
    off = col0 // tn
    return pl.pallas_call(
        _plain_kernel,
        out_shape=jax.ShapeDtypeStruct((m, ncols), BF16),
        grid=(m // tm, ncols // tn),
        in_specs=[pl.BlockSpec((tm, d), lambda i, j: (i, 0)), pl.BlockSpec((d, tn), lambda i, j: (0, j + off))],
        out_specs=pl.BlockSpec((tm, tn), lambda i, j: (i, j)),
        compiler_params=_params("parallel", "arbitrary"),
        name="heads_proj_plain",
    )(h, w)


CONV_HALO = 32
CONV_RC = 128
CONV_CC = 128


def _conv_out_kernel(u_ref, uh_ref, dww_ref, dwb_ref, lg_ref, lb_ref, w_ref, bo_ref, x_ref, g_ref,
                     o_ref, ubuf, cbuf, h_scr, *, width, tiles_per_seq):
    i = pl.program_id(0)
    j = pl.program_id(1)
    tm, d = u_ref.shape

    @pl.when(j == 0)
    def _():
        first = (i % tiles_per_seq) == 0
        ubuf[0:CONV_HALO, :] = jnp.where(first, 0.0, uh_ref[...])
        ubuf[CONV_HALO:CONV_HALO + tm, :] = u_ref[...]
        rc = min(CONV_RC, tm)
        cc = min(CONV_CC, d)
        base = CONV_HALO - (width - 1)

        def row_body(r, carry):
            r0 = pl.multiple_of(r * rc, rc)

            def col_body(c, carry2):
                c0 = pl.multiple_of(c * cc, cc)
                acc = jnp.zeros((rc, cc), F32) + dwb_ref[:, pl.ds(c0, cc)]
                win = ubuf[pl.ds(r0, rc + CONV_HALO), pl.ds(c0, cc)]
                rows = rc + CONV_HALO
                for res in range(SUBLANES):
                    sh = win if res == 0 else pltpu.roll(win, rows - res, axis=0)
                    for t in range(width):
                        if (base + t) % SUBLANES == res:
                            a0 = base + t - res
                            acc = acc + sh[a0:a0 + rc, :] * dww_ref[t:t + 1, pl.ds(c0, cc)]
                cbuf[pl.ds(r0, rc), pl.ds(c0, cc)] = acc
                return carry2

            return lax.fori_loop(0, d // cc, col_body, carry)

        lax.fori_loop(0, tm // rc, row_body, 0)
        c = cbuf[...]
        mu = jnp.mean(c, axis=-1, keepdims=True)
        cen = c - mu
        var = jnp.mean(cen * cen, axis=-1, keepdims=True)
        y = cen * lax.rsqrt(var + NORM_EPS) * lg_ref[...] + lb_ref[...]
        h_scr[...] = (y * _sigmoid(y)).astype(BF16)

    y = _dot(h_scr[...], w_ref[...]) + bo_ref[...]
    o_ref[...] = x_ref[...] + (1.0 + g_ref[0]) * y


def _conv_out(u, x, modv, dw_w, dw_b, ln_g, ln_b, w_out, b_out, seq):
    m, d = u.shape
    width = dw_w.shape[0]
    assert width - 1 <= CONV_HALO
    tm = _pick(seq, 512, CONV_HALO)
    tn = _pick(d, 512, LANES)
    tps = seq // tm
    hb = tm // CONV_HALO
    wpad = -(-width // SUBLANES) * SUBLANES
    dww = jnp.pad(dw_w, ((0, wpad - width), (0, 0)))
    return pl.pallas_call(
        functools.partial(_conv_out_kernel, width=width, tiles_per_seq=tps),
        out_shape=jax.ShapeDtypeStruct((m, d), F32),
        grid=(m // tm, d // tn),
        in_specs=[
            pl.BlockSpec((tm, d), lambda i, j: (i, 0)),
            pl.BlockSpec((CONV_HALO, d), lambda i, j: (jnp.maximum(i * hb - 1, 0), 0)),
            pl.BlockSpec((wpad, d), lambda i, j: (0, 0)),
            pl.BlockSpec((1, d), lambda i, j: (0, 0)),
            pl.BlockSpec((1, d), lambda i, j: (0, 0)),
            pl.BlockSpec((1, d), lambda i, j: (0, 0)),
            pl.BlockSpec((d, tn), lambda i, j: (0, j)),
            pl.BlockSpec((1, tn), lambda i, j: (0, j)),
            pl.BlockSpec((tm, tn), lambda i, j: (i, j)),
            pl.BlockSpec((1, 1, tn), lambda i, j: ((i // tps) * 6 + 2, 0, j)),
        ],
        out_specs=pl.BlockSpec((tm, tn), lambda i, j: (i, j)),
        scratch_shapes=[
            pltpu.VMEM((tm + CONV_HALO, d), F32),
            pltpu.VMEM((tm, d), F32),
            pltpu.VMEM((tm, d), BF16),
        ],
        compiler_params=_params("parallel", "arbitrary"),
        name="conv_out",
    )(u, u, dww, dw_b.reshape(1, d), ln_g.reshape(1, d), ln_b.reshape(1, d), w_out, b_out.reshape(1, d), x, modv)


def _oproj_kernel(a_ref, w_ref, x_ref, g_ref, o_ref):
    tn = o_ref.shape[1]
    sub = min(PROJ_SUB, tn)
    for c in range(tn // sub):
        sl = slice(c * sub, (c + 1) * sub)
        o_ref[:, sl] = x_ref[:, sl] + (1.0 + g_ref[0, :, sl]) * _dot(a_ref[...], w_ref[:, sl])


def _oproj(a, w, x, modv, seq):
    m, d = x.shape
    tm = _pick(seq, 1024, SUBLANES)
    tn = _pick(d, 1024, LANES)
    tps = seq // tm
    return pl.pallas_call(
        _oproj_kernel,
        out_shape=jax.ShapeDtypeStruct((m, d), F32),
        grid=(m // tm, d // tn),
        in_specs=[
            pl.BlockSpec((tm, d), lambda i, j: (i, 0)),
            pl.BlockSpec((d, tn), lambda i, j: (0, j)),
            pl.BlockSpec((tm, tn), lambda i, j: (i, j)),
            pl.BlockSpec((1, 1, tn), lambda i, j: ((i // tps) * 6 + 2, 0, j)),
        ],
        out_specs=pl.BlockSpec((tm, tn), lambda i, j: (i, j)),
        compiler_params=_params("parallel", "arbitrary"),
        name="attn_oproj",
    )(a, w, x, modv)


MLP_OC = 512
MLP_VMEM_LIMIT = 58 * 1024 * 1024


def _mlp_kernel(x_ref, ng_ref, sc_ref, sh_ref, g_ref, w1_ref, w2_ref, o_ref, h_scr):
    f = pl.program_id(1)
    d = o_ref.shape[1]

    @pl.when(f == 0)
    def _():
        _normmod(x_ref, ng_ref, sc_ref, sh_ref, h_scr)
        o_ref[...] = jnp.zeros_like(o_ref)

    tf = w1_ref.shape[1]
    half = tf // 2
    us = []
    for p in range(2):
        r = _dot(h_scr[...], w1_ref[:, p * half:(p + 1) * half])
        us.append(jnp.square(jnp.maximum(r, 0.0)).astype(BF16))
    oc = min(MLP_OC, d)
    for c in range(d // oc):
        sl = slice(c * oc, (c + 1) * oc)
        o_ref[:, sl] += _dot(us[0], w2_ref[0:half, sl]) + _dot(us[1], w2_ref[half:tf, sl])

    @pl.when(f == pl.num_programs(1) - 1)
    def _():
        o_ref[...] = x_ref[...] + (1.0 + g_ref[0]) * o_ref[...]


def _mlp(x, ng, modv, w1, w2, seq):
    m, d = x.shape
    dff = w1.shape[1]
    tm = _pick(seq, 1024, SUBLANES)
    tf = _pick(dff, 1024, LANES)
    tps = seq // tm
    return pl.pallas_call(
        _mlp_kernel,
        out_shape=jax.ShapeDtypeStruct((m, d), F32),
        grid=(m // tm, dff // tf),
        in_specs=[
            pl.BlockSpec((tm, d), lambda i, f: (i, 0)),
            pl.BlockSpec((1, d), lambda i, f: (0, 0)),
            pl.BlockSpec((1, 1, d), lambda i, f: ((i // tps) * 6 + 4, 0, 0)),
            pl.BlockSpec((1, 1, d), lambda i, f: ((i // tps) * 6 + 3, 0, 0)),
            pl.BlockSpec((1, 1, d), lambda i, f: ((i // tps) * 6 + 5, 0, 0)),
            pl.BlockSpec((d, tf), lambda i, f: (0, f)),
            pl.BlockSpec((tf, d), lambda i, f: (f, 0)),
        ],
        out_specs=pl.BlockSpec((tm, d), lambda i, f: (i, 0)),
        scratch_shapes=[pltpu.VMEM((tm, d), BF16)],
        compiler_params=_params("parallel", "arbitrary", vmem_limit=MLP_VMEM_LIMIT),
        name="mlp_fused",
    )(x, ng.reshape(1, d), modv, modv, modv, w1, w2)


def _cumsum_kernel(x_ref, u_ref, e_ref, m_ref, o_ref):
    hi = lax.Precision.HIGHEST
    x = x_ref[0]
    xc = jnp.dot(x, u_ref[...], precision=hi, preferred_element_type=F32)
    tot = jnp.dot(xc, e_ref[...], precision=hi, preferred_element_type=F32)
    o_ref[0] = xc + jnp.dot(m_ref[...], tot, precision=hi, preferred_element_type=F32)


def _cumsum_seq(lf):
    b, h, s = lf.shape
    nblk = s // LANES
    r = h * nblk
    x = lf.reshape(b, r, LANES)
    li = jnp.arange(LANES)
    upper = (li[:, None] <= li[None, :]).astype(F32)
    last = jnp.broadcast_to((li[:, None] == LANES - 1), (LANES, LANES)).astype(F32)
    ri = jnp.arange(r)
    prev = ((ri[:, None] // nblk == ri[None, :] // nblk) & (ri[None, :] < ri[:, None])).astype(F32)
    out = pl.pallas_call(
        _cumsum_kernel,
        out_shape=jax.ShapeDtypeStruct((b, r, LANES), F32),
        grid=(b,),
        in_specs=[
            pl.BlockSpec((1, r, LANES), lambda i: (i, 0, 0)),
            pl.BlockSpec((LANES, LANES), lambda i: (0, 0)),
            pl.BlockSpec((LANES, LANES), lambda i: (0, 0)),
            pl.BlockSpec((r, r), lambda i: (0, 0)),
        ],
        out_specs=pl.BlockSpec((1, r, LANES), lambda i: (i, 0, 0)),
        compiler_params=_params("parallel"),
        name="fox_cumsum",
    )(x, upper, last, prev)
    return out.reshape(b, h, s)


HEADS_PER_STEP = 2
KW = 2 * LANES
VT_ROWS = LANES + BF16_SUBLANES


def _split3(c):
    c1 = c.astype(BF16).astype(F32)
    r1 = c - c1
    c2 = r1.astype(BF16).astype(F32)
    c3 = (r1 - c2).astype(BF16).astype(F32)
    return c1, c2, c3


def _lane_column(tile, h):
    lane = lax.broadcasted_iota(jnp.int32, tile.shape, 1)
    return jnp.sum(jnp.where(lane == h, tile, 0.0), axis=-1, keepdims=True)


def _vt_proj_kernel(h_ref, w_ref, o_ref):
    tm = h_ref.shape[0]
    nheads = o_ref.shape[1]
    per = min(PROJ_SUB // LANES, nheads)
    sub = lax.broadcasted_iota(jnp.int32, (BF16_SUBLANES, tm), 0)
    ones_rows = jnp.where(sub == 0, 1.0, 0.0).astype(BF16)
    for c in range(nheads // per):
        r = _dot(h_ref[...], w_ref[:, c * per * LANES:(c + 1) * per * LANES])
        for hh in range(per):
            o_ref[0, c * per + hh, 0:LANES, :] = r[:, hh * LANES:(hh + 1) * LANES].T.astype(BF16)
            o_ref[0, c * per + hh, LANES:VT_ROWS, :] = ones_rows


def _vt_proj(h, w, col0, bsz, seq, nh):
    m, d = h.shape
    tm = _pick(seq, 1024, LANES)
    tn = _pick(d, 1024, LANES)
    assert col0 % tn == 0
    off = col0 // tn
    tps = seq // tm
    return pl.pallas_call(
        _vt_proj_kernel,
        out_shape=jax.ShapeDtypeStruct((bsz, nh, VT_ROWS, seq), BF16),
        grid=(m // tm, d // tn),
        in_specs=[pl.BlockSpec((tm, d), lambda i, j: (i, 0)), pl.BlockSpec((d, tn), lambda i, j: (0, j + off))],
        out_specs=pl.BlockSpec((1, tn // LANES, VT_ROWS, tm), lambda i, j: (i // tps, j, 0, i % tps)),
        compiler_params=_params("parallel", "arbitrary"),
        name="values_proj_t",
    )(h, w)


def _fox_keys_kernel(k_ref, ct_ref, ka_ref):
    hp = pl.program_id(1)
    ts = k_ref.shape[1]
    lane = lax.broadcasted_iota(jnp.int32, (ts, LANES), 1)
    for hh in range(HEADS_PER_STEP):
        ck = _lane_column(ct_ref[0], hp * HEADS_PER_STEP + hh) * LOG2E
        c1, c2, c3 = _split3(ck)
        extra = jnp.where(lane == 0, -c1, jnp.where(lane == 1, -c2, jnp.where(lane == 2, -c3,
                          jnp.where(lane < 6, 1.0, 0.0))))
        ka_ref[0, hh, :, 0:LANES] = k_ref[0, :, hh * LANES:(hh + 1) * LANES]
        ka_ref[0, hh, :, LANES:KW] = extra.astype(BF16)


def _fox_keys(karr, k_blk, cum_t, bsz, seq, nh):
    ts = _pick(seq, 1024, LANES)
    hps = HEADS_PER_STEP
    return pl.pallas_call(
        _fox_keys_kernel,
        out_shape=jax.ShapeDtypeStruct((bsz, nh, seq, KW), BF16),
        grid=(bsz, nh // hps, seq // ts),
        in_specs=[pl.BlockSpec((1, ts, hps * LANES), lambda b, h, i: (b, i, k_blk + h)),
                  pl.BlockSpec((1, ts, LANES), lambda b, h, i: (b, i, 0))],
        out_specs=pl.BlockSpec((1, hps, ts, KW), lambda b, h, i: (b, h, i, 0)),
        compiler_params=_params("parallel", "parallel", "parallel"),
        name="fox_keys",
    )(karr, cum_t)


def _attn_kernel(*refs, fox, tq, nb_pad, topk):
    if fox:
        q_ref, ka_ref, vt_ref, ct_ref, og_ref, o_ref, qa_scr, m_scr, acc_scr, sa_scr, sb_scr = refs
    else:
        q_ref, ka_ref, vt_ref, ind_ref, o_ref, qa_scr, m_scr, acc_scr, sa_scr, sb_scr, km_scr = refs
    hp = pl.program_id(1)
    i = pl.program_id(2)
    hps = HEADS_PER_STEP

    for hh in range(hps):
        q = q_ref[0, :, hh * LANES:(hh + 1) * LANES]
        if fox:
            lane = lax.broadcasted_iota(jnp.int32, (tq, LANES), 1)
            cq = _lane_column(ct_ref[0], hp * hps + hh) * LOG2E
            c1, c2, c3 = _split3(cq)
            extra = jnp.where(lane < 3, 1.0, jnp.where(lane == 3, c1, jnp.where(lane == 4, c2,
                              jnp.where(lane == 5, c3, 0.0))))
        else:
            @pl.when(i == 0)
            def _():
                km_scr[hh] = (_dot(ind_ref[...], ka_ref[0, :, hh * LANES:(hh + 1) * LANES])
                              * (1.0 / MOBA_BLOCK)).astype(BF16)

            gate = _dot_t(km_scr[hh, 0:nb_pad, :], q)
            blk = lax.broadcasted_iota(jnp.int32, (nb_pad, tq), 0)
            blk_f = blk.astype(F32)
            qpos = lax.broadcasted_iota(jnp.int32, (nb_pad, tq), 1) + i * tq
            cur = lax.shift_right_logical(qpos, int(math.log2(MOBA_BLOCK)))
            valid = blk < cur
            gate = jnp.where(valid, gate, NEG_INF)
            picked = jnp.zeros((nb_pad, tq), F32)
            for _ in range(topk):
                mx = jnp.max(gate, axis=0, keepdims=True)
                first = jnp.min(jnp.where(gate == mx, blk_f, float(nb_pad)), axis=0, keepdims=True)
                one = blk_f == first
                picked = jnp.where(one, 1.0, picked)
                gate = jnp.where(one, -jnp.inf, gate)
            allowed = ((picked > 0.0) & valid) | (blk == cur)
            sel_t = jnp.where(allowed, 0.0, NEG_INF)
            if nb_pad < LANES:
                sel_t = jnp.concatenate([sel_t, jnp.zeros((LANES - nb_pad, tq), F32)], axis=0)
            extra = sel_t.T
        qa_scr[hh, :, 0:LANES] = q
        qa_scr[hh, :, LANES:KW] = extra.astype(BF16)
        m_scr[hh] = jnp.full((SUBLANES, tq), NEG_INF, F32)
        acc_scr[hh] = jnp.zeros((VT_ROWS, tq), F32)

    def fill(s_buf, j):
        k0 = pl.multiple_of(j * tq, tq)
        if not fox:
            kpos = lax.broadcasted_iota(jnp.int32, (tq, LANES), 0) + k0
            lane = lax.broadcasted_iota(jnp.int32, (tq, LANES), 1)
            onehot = jnp.where(lane == lax.shift_right_logical(kpos, int(math.log2(MOBA_BLOCK))), 1.0, 0.0).astype(BF16)
        for hh in range(hps):
            if fox:
                kt = ka_ref[0, hh, pl.ds(k0, tq), :]
            else:
                kt = jnp.concatenate([ka_ref[0, pl.ds(k0, tq), hh * LANES:(hh + 1) * LANES], onehot], axis=1)
            s_buf[hh] = _dot_t(kt, qa_scr[hh])

    def consume(s_buf, j, masked):
        k0 = pl.multiple_of(j * tq, tq)
        for hh in range(hps):
            vt = vt_ref[0, hh, :, pl.ds(k0, tq)]
            s = s_buf[hh]
            if masked:
                key = lax.broadcasted_iota(jnp.int32, (tq, tq), 0)
                qry = lax.broadcasted_iota(jnp.int32, (tq, tq), 1)
                s = jnp.where(key <= qry, s, NEG_INF)
            m_prev = m_scr[hh, 0:1, :]
            m_new = jnp.maximum(m_prev, jnp.max(s, axis=0, keepdims=True))
            alpha = jnp.exp2(m_prev - m_new)
            p = jnp.exp2(s - m_new).astype(BF16)
            acc_scr[hh] = alpha * acc_scr[hh] + _dot(vt, p)
            m_scr[hh] = jnp.broadcast_to(m_new, (SUBLANES, tq))

    fill(sa_scr, 0)
    npairs = i // 2
    nquads = npairs // 2

    def pair_at(j0):
        fill(sb_scr, j0 + 1)
        consume(sa_scr, j0, False)
        fill(sa_scr, j0 + 2)
        consume(sb_scr, j0 + 1, False)

    def quad(t, carry):
        pair_at(4 * t)
        pair_at(4 * t + 2)
        return carry

    lax.fori_loop(0, nquads, quad, 0)

    def pair(t, carry):
        pair_at(2 * t)
        return carry

    lax.fori_loop(2 * nquads, npairs, pair, 0)

    @pl.when(i == 2 * npairs)
    def _():
        consume(sa_scr, i, True)

    @pl.when(i != 2 * npairs)
    def _():
        fill(sb_scr, i)
        consume(sa_scr, i - 1, False)
        consume(sb_scr, i, True)

    for hh in range(hps):
        acc = acc_scr[hh]
        o_t = acc[0:LANES, :] / acc[LANES:LANES + 1, :]
        out = o_t.T
        if fox:
            out = out * _sigmoid(og_ref[0, :, hh * LANES:(hh + 1) * LANES].astype(F32))
        o_ref[0, :, hh * LANES:(hh + 1) * LANES] = out.astype(BF16)


def _attention(arr, q_blk, k_blk, ka, vt, cum_t, bsz, seq, nh, gate_arr=None, gate_blk=0):
    fox = cum_t is not None
    hps = HEADS_PER_STEP
    d = nh * LANES
    tq = _pick(seq, 512, LANES)
    nb = seq // MOBA_BLOCK
    nb_pad = -(-nb // SUBLANES) * SUBLANES
    if fox:
        key_spec = pl.BlockSpec((1, hps, seq, KW), lambda b, h, i: (b, h, 0, 0))
    else:
        key_spec = pl.BlockSpec((1, seq, hps * LANES), lambda b, h, i: (b, 0, k_blk + h))
        ka = arr
    in_specs = [
        pl.BlockSpec((1, tq, hps * LANES), lambda b, h, i: (b, i, q_blk + h)),
        key_spec,
        pl.BlockSpec((1, hps, VT_ROWS, seq), lambda b, h, i: (b, h, 0, 0)),
    ]
    args = [arr, ka, vt]
    scratch = [pltpu.VMEM((hps, tq, KW), BF16), pltpu.VMEM((hps, SUBLANES, tq), F32),
               pltpu.VMEM((hps, VT_ROWS, tq), F32), pltpu.VMEM((hps, tq, tq), F32),
               pltpu.VMEM((hps, tq, tq), F32)]
    if fox:
        in_specs.append(pl.BlockSpec((1, tq, LANES), lambda b, h, i: (b, i, 0)))
        in_specs.append(pl.BlockSpec((1, tq, hps * LANES), lambda b, h, i: (b, i, gate_blk + h)))
        args += [cum_t, gate_arr]
    else:
        assert seq % MOBA_BLOCK == 0 and nb_pad <= LANES and tq % MOBA_BLOCK == 0
        ind = (jnp.arange(LANES)[:, None] == (jnp.arange(seq) // MOBA_BLOCK)[None, :]).astype(BF16)
        in_specs.append(pl.BlockSpec((LANES, seq), lambda b, h, i: (0, 0)))
        args.append(ind)
        scratch.append(pltpu.VMEM((hps, LANES, LANES), BF16))
    out = pl.pallas_call(
        functools.partial(_attn_kernel, fox=fox, tq=tq, nb_pad=nb_pad, topk=MOBA_TOPK),
        out_shape=jax.ShapeDtypeStruct((bsz, seq, d), BF16),
        grid=(bsz, nh // hps, seq // tq),
        in_specs=in_specs,
        out_specs=pl.BlockSpec((1, tq, hps * LANES), lambda b, h, i: (b, i, h)),
        scratch_shapes=scratch,
        compiler_params=_params("parallel", "parallel", "arbitrary"),
        name="fox_attention" if fox else "moba_attention",
    )(*args)
    return out.reshape(bsz * seq, d)


def kernel(x, c, norm1_g, norm2_g, mod_w, mod_b, mlp_w1, mlp_w2, conv_w_in, conv_b_in, conv_dw_w, conv_dw_b,
           conv_ln_g, conv_ln_b, conv_w_out, conv_b_out, moba_w_qkv, moba_q_g, moba_k_g, moba_w_o, fox_w_in,
           fox_b_f, fox_q_g, fox_k_g, fox_w_o):
    bsz, seq, d = x.shape
    depth = mod_w.shape[0]
    hd = moba_q_g.shape[-1]
    nh = d // hd
    assert hd == LANES and nh <= LANES and nh % HEADS_PER_STEP == 0
    hps = HEADS_PER_STEP
    qscale = hd ** -0.5 * LOG2E

    mod = _mod_vectors(c, mod_w, mod_b)
    xf = x.reshape(bsz * seq, d)

    for i in range(depth):
        kind = i % N_MIXERS
        jx = i // N_MIXERS
        modv = mod[i].reshape(bsz * 6, 1, d)
        if kind == 0:
            u = _conv_in(xf, norm1_g[i], modv, conv_w_in[jx].astype(BF16), conv_b_in[jx], seq)
            xf = _conv_out(u, xf, modv, conv_dw_w[jx], conv_dw_b[jx], conv_ln_g[jx], conv_ln_b[jx],
                           conv_w_out[jx].astype(BF16), conv_b_out[jx], seq)
        elif kind == 1:
            w = moba_w_qkv[jx].astype(BF16)
            gvec = jnp.concatenate([jnp.tile(moba_q_g[jx], nh) * qscale, jnp.tile(moba_k_g[jx], nh)]).reshape(1, 2 * d)
            qk, h = _heads_proj(xf, norm1_g[i], modv, w, 2 * d, gvec, seq)
            vt = _vt_proj(h, w, 2 * d, bsz, seq, nh)
            qk = qk.reshape(bsz, seq, 2 * d)
            o = _attention(qk, 0, nh // hps, None, vt, None, bsz, seq, nh)
            xf = _oproj(o, moba_w_o[jx].astype(BF16), xf, modv, seq)
        else:
            w_in = fox_w_in[jx]
            w = w_in[:, :4 * d].astype(BF16)
            gvec = jnp.concatenate([jnp.tile(fox_q_g[jx], nh) * qscale, jnp.tile(fox_k_g[jx], nh)]).reshape(1, 2 * d)
            wf = jnp.pad(w_in[:, 4 * d:], ((0, 0), (0, LANES - nh))).astype(BF16)
            bf = jnp.pad(fox_b_f[jx], (0, LANES - nh)).reshape(1, LANES)
            qk, h, lf = _heads_proj(xf, norm1_g[i], modv, w, 2 * d, gvec, seq, wf, bf)
            vt = _vt_proj(h, w, 2 * d, bsz, seq, nh)
            og = _plain_proj(h, w, 3 * d, d, seq).reshape(bsz, seq, d)
            lf = lf[:, :nh].reshape(bsz, seq, nh).transpose(0, 2, 1)
            cum = _cumsum_seq(lf)
            cum_t = jnp.pad(cum.transpose(0, 2, 1), ((0, 0), (0, 0), (0, LANES - nh)))
            qk = qk.reshape(bsz, seq, 2 * d)
            ka = _fox_keys(qk, nh // hps, cum_t, bsz, seq, nh)
            o = _attention(qk, 0, 0, ka, vt, cum_t, bsz, seq, nh, gate_arr=og, gate_blk=0)
            xf = _oproj(o, fox_w_o[jx].astype(BF16), xf, modv, seq)
        xf = _mlp(xf, norm2_g[i], modv, mlp_w1[i].astype(BF16), mlp_w2[i].astype(BF16), seq)
    return xf.reshape(bsz, seq, d)
```

```python
import functools
import math

import jax
import jax.numpy as jnp
from jax import lax
from jax.experimental import pallas as pl
from jax.experimental.pallas import tpu as pltpu

N_MIXERS = 3
MOBA_BLOCK = 256
MOBA_TOPK = 3
NORM_EPS = 1e-6
NEG_INF = -1e30

LANES = 128
SUBLANES = 8
BF16_SUBLANES = 16
VMEM_LIMIT = 56 * 1024 * 1024

LOG2E = math.log2(math.e)

BF16 = jnp.bfloat16
F32 = jnp.float32


def _pick(n, pref, mult):
    if n <= pref:
        return n
    t = (pref // mult) * mult
    while t > mult and n % t:
        t -= mult
    assert n % t == 0, (n, pref, mult)
    return t


def _params(*sem, vmem_limit=VMEM_LIMIT):
    return pltpu.CompilerParams(dimension_semantics=sem, vmem_limit_bytes=vmem_limit)


def _dot(a, b):
    return jnp.dot(a, b, preferred_element_type=F32)


def _dot_t(a, b):
    return lax.dot_general(a, b, (((1,), (1,)), ((), ())), preferred_element_type=F32)


def _sigmoid(x):
    return 1.0 / (1.0 + jnp.exp(-x))


def _log_sigmoid(x):
    return -(jnp.maximum(-x, 0.0) + jnp.log1p(jnp.exp(-jnp.abs(x))))


NORM_ROWS = 32


def _normmod(x_ref, ng_ref, sc_ref, sh_ref, h_ref):
    tm = x_ref.shape[0]
    rows = min(NORM_ROWS, tm)
    ng = ng_ref[...]
    sc1 = 1.0 + sc_ref[0]
    sh = sh_ref[0]
    for c in range(tm // rows):
        sl = slice(c * rows, (c + 1) * rows)
        x = x_ref[sl, :]
        ms = jnp.mean(x * x, axis=-1, keepdims=True)
        y = x * lax.rsqrt(ms + NORM_EPS) * ng
        h_ref[sl, :] = (y * sc1 + sh).astype(h_ref.dtype)


def _mod_kernel(c_ref, w_ref, b_ref, o_ref):
    c = c_ref[...]
    cond = c * _sigmoid(c)
    o_ref[0] = _dot(cond, w_ref[0]) + b_ref[0]


def _mod_vectors(c, mod_w, mod_b):
    depth, d, n = mod_w.shape
    b = c.shape[0]
    bp = -(-b // SUBLANES) * SUBLANES
    cp = jnp.pad(c, ((0, bp - b), (0, 0)))
    tn = _pick(n, 1024, LANES)
    out = pl.pallas_call(
        _mod_kernel,
        out_shape=jax.ShapeDtypeStruct((depth, bp, n), F32),
        grid=(depth, n // tn),
        in_specs=[
            pl.BlockSpec((bp, d), lambda i, j: (0, 0)),
            pl.BlockSpec((1, d, tn), lambda i, j: (i, 0, j)),
            pl.BlockSpec((1, 1, tn), lambda i, j: (i, 0, j)),
        ],
        out_specs=pl.BlockSpec((1, bp, tn), lambda i, j: (i, 0, j)),
        compiler_params=_params("parallel", "parallel"),
        name="mod_vectors",
    )(cp, mod_w, mod_b.reshape(depth, 1, n))
    return out[:, :b]


PROJ_SUB = 512


def _glu_kernel(x_ref, ng_ref, sc_ref, sh_ref, wa_ref, wg_ref, ba_ref, bg_ref, o_ref, h_scr):
    @pl.when(pl.program_id(1) == 0)
    def _():
        _normmod(x_ref, ng_ref, sc_ref, sh_ref, h_scr)

    tn = o_ref.shape[1]
    sub = min(PROJ_SUB, tn)
    for c in range(tn // sub):
        sl = slice(c * sub, (c + 1) * sub)
        a = _dot(h_scr[...], wa_ref[:, sl]) + ba_ref[:, sl]
        g = _dot(h_scr[...], wg_ref[:, sl]) + bg_ref[:, sl]
        o_ref[:, sl] = a * _sigmoid(g)


def _conv_in(x, ng, modv, w_in, b_in, seq):
    m, d = x.shape
    tm = _pick(seq, 1024, SUBLANES)
    tn = _pick(d, 1024, LANES)
    nj = d // tn
    tps = seq // tm
    b2 = b_in.reshape(1, 2 * d)
    return pl.pallas_call(
        _glu_kernel,
        out_shape=jax.ShapeDtypeStruct((m, d), F32),
        grid=(m // tm, nj),
        in_specs=[
            pl.BlockSpec((tm, d), lambda i, j: (i, 0)),
            pl.BlockSpec((1, d), lambda i, j: (0, 0)),
            pl.BlockSpec((1, 1, d), lambda i, j: ((i // tps) * 6 + 1, 0, 0)),
            pl.BlockSpec((1, 1, d), lambda i, j: ((i // tps) * 6 + 0, 0, 0)),
            pl.BlockSpec((d, tn), lambda i, j: (0, j)),
            pl.BlockSpec((d, tn), lambda i, j: (0, j + nj)),
            pl.BlockSpec((1, tn), lambda i, j: (0, j)),
            pl.BlockSpec((1, tn), lambda i, j: (0, j + nj)),
        ],
        out_specs=pl.BlockSpec((tm, tn), lambda i, j: (i, j)),
        scratch_shapes=[pltpu.VMEM((tm, d), BF16)],
        compiler_params=_params("parallel", "arbitrary"),
        name="conv_in_glu",
    )(x, ng.reshape(1, d), modv, modv, w_in, w_in, b2, b2)


def _heads_kernel(x_ref, ng_ref, sc_ref, sh_ref, w_ref, gv_ref, *rest, with_gate):
    if with_gate:
        wf_ref, bf_ref, o_ref, h_ref, lf_ref = rest
    else:
        o_ref, h_ref = rest

    @pl.when(pl.program_id(1) == 0)
    def _():
        _normmod(x_ref, ng_ref, sc_ref, sh_ref, h_ref)
        if with_gate:
            lf_ref[...] = _log_sigmoid(_dot(h_ref[...], wf_ref[...]) + bf_ref[...])

    tn = o_ref.shape[1]
    sub = min(PROJ_SUB, tn)
    for c in range(tn // sub):
        r = _dot(h_ref[...], w_ref[:, c * sub:(c + 1) * sub])
        for hh in range(sub // LANES):
            sl = slice(c * sub + hh * LANES, c * sub + (hh + 1) * LANES)
            rh = r[:, hh * LANES:(hh + 1) * LANES]
            ms = jnp.mean(rh * rh, axis=-1, keepdims=True)
            o_ref[:, sl] = (rh * lax.rsqrt(ms + NORM_EPS) * gv_ref[:, sl]).astype(BF16)


def _heads_proj(x, ng, modv, w, ncols, gvec, seq, wf=None, bf=None):
    m, d = x.shape
    tm = _pick(seq, 1024, SUBLANES)
    tn = _pick(ncols, 1024, LANES)
    tps = seq // tm
    with_gate = wf is not None
    in_specs = [
        pl.BlockSpec((tm, d), lambda i, j: (i, 0)),
        pl.BlockSpec((1, d), lambda i, j: (0, 0)),
        pl.BlockSpec((1, 1, d), lambda i, j: ((i // tps) * 6 + 1, 0, 0)),
        pl.BlockSpec((1, 1, d), lambda i, j: ((i // tps) * 6 + 0, 0, 0)),
        pl.BlockSpec((d, tn), lambda i, j: (0, j)),
        pl.BlockSpec((1, tn), lambda i, j: (0, j)),
    ]
    args = [x, ng.reshape(1, d), modv, modv, w, gvec]
    out_shape = [jax.ShapeDtypeStruct((m, ncols), BF16), jax.ShapeDtypeStruct((m, d), BF16)]
    out_specs = [pl.BlockSpec((tm, tn), lambda i, j: (i, j)), pl.BlockSpec((tm, d), lambda i, j: (i, 0))]
    if with_gate:
        in_specs += [pl.BlockSpec((d, LANES), lambda i, j: (0, 0)), pl.BlockSpec((1, LANES), lambda i, j: (0, 0))]
        args += [wf, bf]
        out_shape.append(jax.ShapeDtypeStruct((m, LANES), F32))
        out_specs.append(pl.BlockSpec((tm, LANES), lambda i, j: (i, 0)))
    return pl.pallas_call(
        functools.partial(_heads_kernel, with_gate=with_gate),
        out_shape=out_shape,
        grid=(m // tm, ncols // tn),
        in_specs=in_specs,
        out_specs=out_specs,
        compiler_params=_params("parallel", "arbitrary"),
        name="heads_proj_norm" + ("_gate" if with_gate else ""),
    )(*args)


def _plain_kernel(h_ref, w_ref, o_ref):
    tn = o_ref.shape[1]
    sub = min(PROJ_SUB, tn)
    for c in range(tn // sub):
        sl = slice(c * sub, (c + 1) * sub)
        o_ref[:, sl] = _dot(h_ref[...], w_ref[:, sl]).astype(BF16)


def _plain_proj(h, w, col0, ncols, seq):
    m, d = h.shape
    tm = _pick(seq, 1024, SUBLANES)
    tn = _pick(ncols, 1024, LANES)
    assert col0 % tn == 0
    off = col0 // tn
    return pl.pallas_call(
        _plain_kernel,
        out_shape=jax.ShapeDtypeStruct((m, ncols), BF16),
        grid=(m // tm, ncols // tn),
        in_specs=[pl.BlockSpec((tm, d), lambda i, j: (i, 0)), pl.BlockSpec((d, tn), lambda i, j: (0, j + off))],
        out_specs=pl.BlockSpec((tm, tn), lambda i, j: (i, j)),
        compiler_params=_params("parallel", "arbitrary"),
        name="heads_proj_plain",
    )(h, w)


CONV_HALO = 32
CONV_RC = 128
CONV_CC = 128


def _conv_out_kernel(u_ref, uh_ref, dww_ref, dwb_ref, lg_ref, lb_ref, w_ref, bo_ref, x_ref, g_ref,
                     o_ref, ubuf, cbuf, h_scr, *, width, tiles_per_seq):
    i = pl.program_id(0)
    j = pl.program_id(1)
    tm, d = u_ref.shape

    @pl.when(j == 0)
    def _():
        first = (i % tiles_per_seq) == 0
        ubuf[0:CONV_HALO, :] = jnp.where(first, 0.0, uh_ref[...])
        ubuf[CONV_HALO:CONV_HALO + tm, :] = u_ref[...]
        rc = min(CONV_RC, tm)
        cc = min(CONV_CC, d)
        base = CONV_HALO - (width - 1)

        def row_body(r, carry):
            r0 = pl.multiple_of(r * rc, rc)

            def col_body(c, carry2):
                c0 = pl.multiple_of(c * cc, cc)
                acc = jnp.zeros((rc, cc), F32) + dwb_ref[:, pl.ds(c0, cc)]
                win = ubuf[pl.ds(r0, rc + CONV_HALO), pl.ds(c0, cc)]
                rows = rc + CONV_HALO
                for res in range(SUBLANES):
                    sh = win if res == 0 else pltpu.roll(win, rows - res, axis=0)
                    for t in range(width):
                        if (base + t) % SUBLANES == res:
                            a0 = base + t - res
                            acc = acc + sh[a0:a0 + rc, :] * dww_ref[t:t + 1, pl.ds(c0, cc)]
                cbuf[pl.ds(r0, rc), pl.ds(c0, cc)] = acc
                return carry2

            return lax.fori_loop(0, d // cc, col_body, carry)

        lax.fori_loop(0, tm // rc, row_body, 0)
        c = cbuf[...]
        mu = jnp.mean(c, axis=-1, keepdims=True)
        cen = c - mu
        var = jnp.mean(cen * cen, axis=-1, keepdims=True)
        y = cen * lax.rsqrt(var + NORM_EPS) * lg_ref[...] + lb_ref[...]
        h_scr[...] = (y * _sigmoid(y)).astype(BF16)

    y = _dot(h_scr[...], w_ref[...]) + bo_ref[...]
    o_ref[...] = x_ref[...] + (1.0 + g_ref[0]) * y


def _conv_out(u, x, modv, dw_w, dw_b, ln_g, ln_b, w_out, b_out, seq):
    m, d = u.shape
    width = dw_w.shape[0]
    assert width - 1 <= CONV_HALO
    tm = _pick(seq, 512, CONV_HALO)
    tn = _pick(d, 512, LANES)
    tps = seq // tm
    hb = tm // CONV_HALO
    wpad = -(-width // SUBLANES) * SUBLANES
    dww = jnp.pad(dw_w, ((0, wpad - width), (0, 0)))
    return pl.pallas_call(
        functools.partial(_conv_out_kernel, width=width, tiles_per_seq=tps),
        out_shape=jax.ShapeDtypeStruct((m, d), F32),
        grid=(m // tm, d // tn),
        in_specs=[
            pl.BlockSpec((tm, d), lambda i, j: (i, 0)),
            pl.BlockSpec((CONV_HALO, d), lambda i, j: (jnp.maximum(i * hb - 1, 0), 0)),
            pl.BlockSpec((wpad, d), lambda i, j: (0, 0)),
            pl.BlockSpec((1, d), lambda i, j: (0, 0)),
            pl.BlockSpec((1, d), lambda i, j: (0, 0)),
            pl.BlockSpec((1, d), lambda i, j: (0, 0)),
            pl.BlockSpec((d, tn), lambda i, j: (0, j)),
            pl.BlockSpec((1, tn), lambda i, j: (0, j)),
            pl.BlockSpec((tm, tn), lambda i, j: (i, j)),
            pl.BlockSpec((1, 1, tn), lambda i, j: ((i // tps) * 6 + 2, 0, j)),
        ],
        out_specs=pl.BlockSpec((tm, tn), lambda i, j: (i, j)),
        scratch_shapes=[
            pltpu.VMEM((tm + CONV_HALO, d), F32),
            pltpu.VMEM((tm, d), F32),
            pltpu.VMEM((tm, d), BF16),
        ],
        compiler_params=_params("parallel", "arbitrary"),
        name="conv_out",
    )(u, u, dww, dw_b.reshape(1, d), ln_g.reshape(1, d), ln_b.reshape(1, d), w_out, b_out.reshape(1, d), x, modv)


def _oproj_kernel(a_ref, w_ref, x_ref, g_ref, o_ref):
    tn = o_ref.shape[1]
    sub = min(PROJ_SUB, tn)
    for c in range(tn // sub):
        sl = slice(c * sub, (c + 1) * sub)
        o_ref[:, sl] = x_ref[:, sl] + (1.0 + g_ref[0, :, sl]) * _dot(a_ref[...], w_ref[:, sl])


def _oproj(a, w, x, modv, seq):
    m, d = x.shape
    tm = _pick(seq, 1024, SUBLANES)
    tn = _pick(d, 1024, LANES)
    tps = seq // tm
    return pl.pallas_call(
        _oproj_kernel,
        out_shape=jax.ShapeDtypeStruct((m, d), F32),
        grid=(m // tm, d // tn),
        in_specs=[
            pl.BlockSpec((tm, d), lambda i, j: (i, 0)),
            pl.BlockSpec((d, tn), lambda i, j: (0, j)),
            pl.BlockSpec((tm, tn), lambda i, j: (i, j)),
            pl.BlockSpec((1, 1, tn), lambda i, j: ((i // tps) * 6 + 2, 0, j)),
        ],
        out_specs=pl.BlockSpec((tm, tn), lambda i, j: (i, j)),
        compiler_params=_params("parallel", "arbitrary"),
        name="attn_oproj",
    )(a, w, x, modv)


MLP_OC = 512
MLP_VMEM_LIMIT = 58 * 1024 * 1024


def _mlp_kernel(x_ref, ng_ref, sc_ref, sh_ref, g_ref, w1_ref, w2_ref, o_ref, h_scr):
    f = pl.program_id(1)
    d = o_ref.shape[1]

    @pl.when(f == 0)
    def _():
        _normmod(x_ref, ng_ref, sc_ref, sh_ref, h_scr)
        o_ref[...] = jnp.zeros_like(o_ref)

    tf = w1_ref.shape[1]
    half = tf // 2
    us = []
    for p in range(2):
        r = _dot(h_scr[...], w1_ref[:, p * half:(p + 1) * half])
        us.append(jnp.square(jnp.maximum(r, 0.0)).astype(BF16))
    oc = min(MLP_OC, d)
    for c in range(d // oc):
        sl = slice(c * oc, (c + 1) * oc)
        o_ref[:, sl] += _dot(us[0], w2_ref[0:half, sl]) + _dot(us[1], w2_ref[half:tf, sl])

    @pl.when(f == pl.num_programs(1) - 1)
    def _():
        o_ref[...] = x_ref[...] + (1.0 + g_ref[0]) * o_ref[...]


def _mlp(x, ng, modv, w1, w2, seq):
    m, d = x.shape
    dff = w1.shape[1]
    tm = _pick(seq, 1024, SUBLANES)
    tf = _pick(dff, 1024, LANES)
    tps = seq // tm
    return pl.pallas_call(
        _mlp_kernel,
        out_shape=jax.ShapeDtypeStruct((m, d), F32),
        grid=(m // tm, dff // tf),
        in_specs=[
            pl.BlockSpec((tm, d), lambda i, f: (i, 0)),
            pl.BlockSpec((1, d), lambda i, f: (0, 0)),
            pl.BlockSpec((1, 1, d), lambda i, f: ((i // tps) * 6 + 4, 0, 0)),
            pl.BlockSpec((1, 1, d), lambda i, f: ((i // tps) * 6 + 3, 0, 0)),
            pl.BlockSpec((1, 1, d), lambda i, f: ((i // tps) * 6 + 5, 0, 0)),
            pl.BlockSpec((d, tf), lambda i, f: (0, f)),
            pl.BlockSpec((tf, d), lambda i, f: (f, 0)),
        ],
        out_specs=pl.BlockSpec((tm, d), lambda i, f: (i, 0)),
        scratch_shapes=[pltpu.VMEM((tm, d), BF16)],
        compiler_params=_params("parallel", "arbitrary", vmem_limit=MLP_VMEM_LIMIT),
        name="mlp_fused",
    )(x, ng.reshape(1, d), modv, modv, modv, w1, w2)


def _cumsum_kernel(x_ref, u_ref, e_ref, m_ref, o_ref):
    hi = lax.Precision.HIGHEST
    x = x_ref[0]
    xc = jnp.dot(x, u_ref[...], precision=hi, preferred_element_type=F32)
    tot = jnp.dot(xc, e_ref[...], precision=hi, preferred_element_type=F32)
    o_ref[0] = xc + jnp.dot(m_ref[...], tot, precision=hi, preferred_element_type=F32)


def _cumsum_seq(lf):
    b, h, s = lf.shape
    nblk = s // LANES
    r = h * nblk
    x = lf.reshape(b, r, LANES)
    li = jnp.arange(LANES)
    upper = (li[:, None] <= li[None, :]).astype(F32)
    last = jnp.broadcast_to((li[:, None] == LANES - 1), (LANES, LANES)).astype(F32)
    ri = jnp.arange(r)
    prev = ((ri[:, None] // nblk == ri[None, :] // nblk) & (ri[None, :] < ri[:, None])).astype(F32)
    out = pl.pallas_call(
        _cumsum_kernel,
        out_shape=jax.ShapeDtypeStruct((b, r, LANES), F32),
        grid=(b,),
        in_specs=[
            pl.BlockSpec((1, r, LANES), lambda i: (i, 0, 0)),
            pl.BlockSpec((LANES, LANES), lambda i: (0, 0)),
            pl.BlockSpec((LANES, LANES), lambda i: (0, 0)),
            pl.BlockSpec((r, r), lambda i: (0, 0)),
        ],
        out_specs=pl.BlockSpec((1, r, LANES), lambda i: (i, 0, 0)),
        compiler_params=_params("parallel"),
        name="fox_cumsum",
    )(x, upper, last, prev)
    return out.reshape(b, h, s)


HEADS_PER_STEP = 2
KW = 2 * LANES
VT_ROWS = LANES + BF16_SUBLANES


def _split3(c):
    c1 = c.astype(BF16).astype(F32)
    r1 = c - c1
    c2 = r1.astype(BF16).astype(F32)
    c3 = (r1 - c2).astype(BF16).astype(F32)
    return c1, c2, c3


def _lane_column(tile, h):
    lane = lax.broadcasted_iota(jnp.int32, tile.shape, 1)
    return jnp.sum(jnp.where(lane == h, tile, 0.0), axis=-1, keepdims=True)


def _vt_proj_kernel(h_ref, w_ref, o_ref):
    tm = h_ref.shape[0]
    nheads = o_ref.shape[1]
    per = min(PROJ_SUB // LANES, nheads)
    sub = lax.broadcasted_iota(jnp.int32, (BF16_SUBLANES, tm), 0)
    ones_rows = jnp.where(sub == 0, 1.0, 0.0).astype(BF16)
    for c in range(nheads // per):
        r = _dot(h_ref[...], w_ref[:, c * per * LANES:(c + 1) * per * LANES])
        for hh in range(per):
            o_ref[0, c * per + hh, 0:LANES, :] = r[:, hh * LANES:(hh + 1) * LANES].T.astype(BF16)
            o_ref[0, c * per + hh, LANES:VT_ROWS, :] = ones_rows


def _vt_proj(h, w, col0, bsz, seq, nh):
    m, d = h.shape
    tm = _pick(seq, 1024, LANES)
    tn = _pick(d, 1024, LANES)
    assert col0 % tn == 0
    off = col0 // tn
    tps = seq // tm
    return pl.pallas_call(
        _vt_proj_kernel,
        out_shape=jax.ShapeDtypeStruct((bsz, nh, VT_ROWS, seq), BF16),
        grid=(m // tm, d // tn),
        in_specs=[pl.BlockSpec((tm, d), lambda i, j: (i, 0)), pl.BlockSpec((d, tn), lambda i, j: (0, j + off))],
        out_specs=pl.BlockSpec((1, tn // LANES, VT_ROWS, tm), lambda i, j: (i // tps, j, 0, i % tps)),
        compiler_params=_params("parallel", "arbitrary"),
        name="values_proj_t",
    )(h, w)


def _fox_keys_kernel(k_ref, ct_ref, ka_ref):
    hp = pl.program_id(1)
    ts = k_ref.shape[1]
    lane = lax.broadcasted_iota(jnp.int32, (ts, LANES), 1)
    for hh in range(HEADS_PER_STEP):
        ck = _lane_column(ct_ref[0], hp * HEADS_PER_STEP + hh) * LOG2E
        c1, c2, c3 = _split3(ck)
        extra = jnp.where(lane == 0, -c1, jnp.where(lane == 1, -c2, jnp.where(lane == 2, -c3,
                          jnp.where(lane < 6, 1.0, 0.0))))
        ka_ref[0, hh, :, 0:LANES] = k_ref[0, :, hh * LANES:(hh + 1) * LANES]
        ka_ref[0, hh, :, LANES:KW] = extra.astype(BF16)


def _fox_keys(karr, k_blk, cum_t, bsz, seq, nh):
    ts = _pick(seq, 1024, LANES)
    hps = HEADS_PER_STEP
    return pl.pallas_call(
        _fox_keys_kernel,
        out_shape=jax.ShapeDtypeStruct((bsz, nh, seq, KW), BF16),
        grid=(bsz, nh // hps, seq // ts),
        in_specs=[pl.BlockSpec((1, ts, hps * LANES), lambda b, h, i: (b, i, k_blk + h)),
                  pl.BlockSpec((1, ts, LANES), lambda b, h, i: (b, i, 0))],
        out_specs=pl.BlockSpec((1, hps, ts, KW), lambda b, h, i: (b, h, i, 0)),
        compiler_params=_params("parallel", "parallel", "parallel"),
        name="fox_keys",
    )(karr, cum_t)


def _attn_kernel(*refs, fox, tq, nb_pad, topk):
    if fox:
        q_ref, ka_ref, vt_ref, ct_ref, og_ref, o_ref, qa_scr, m_scr, acc_scr, sa_scr, sb_scr = refs
    else:
        q_ref, ka_ref, vt_ref, ind_ref, o_ref, qa_scr, m_scr, acc_scr, sa_scr, sb_scr, km_scr = refs
    hp = pl.program_id(1)
    i = pl.program_id(2)
    hps = HEADS_PER_STEP

    for hh in range(hps):
        q = q_ref[0, :, hh * LANES:(hh + 1) * LANES]
        if fox:
            lane = lax.broadcasted_iota(jnp.int32, (tq, LANES), 1)
            cq = _lane_column(ct_ref[0], hp * hps + hh) * LOG2E
            c1, c2, c3 = _split3(cq)
            extra = jnp.where(lane < 3, 1.0, jnp.where(lane == 3, c1, jnp.where(lane == 4, c2,
                              jnp.where(lane == 5, c3, 0.0))))
        else:
            @pl.when(i == 0)
            def _():
                km_scr[hh] = (_dot(ind_ref[...], ka_ref[0, :, hh * LANES:(hh + 1) * LANES])
                              * (1.0 / MOBA_BLOCK)).astype(BF16)

            gate = _dot_t(km_scr[hh, 0:nb_pad, :], q)
            blk = lax.broadcasted_iota(jnp.int32, (nb_pad, tq), 0)
            blk_f = blk.astype(F32)
            qpos = lax.broadcasted_iota(jnp.int32, (nb_pad, tq), 1) + i * tq
            cur = lax.shift_right_logical(qpos, int(math.log2(MOBA_BLOCK)))
            valid = blk < cur
            gate = jnp.where(valid, gate, NEG_INF)
            picked = jnp.zeros((nb_pad, tq), F32)
            for _ in range(topk):
                mx = jnp.max(gate, axis=0, keepdims=True)
                first = jnp.min(jnp.where(gate == mx, blk_f, float(nb_pad)), axis=0, keepdims=True)
                one = blk_f == first
                picked = jnp.where(one, 1.0, picked)
                gate = jnp.where(one, -jnp.inf, gate)
            allowed = ((picked > 0.0) & valid) | (blk == cur)
            sel_t = jnp.where(allowed, 0.0, NEG_INF)
            if nb_pad < LANES:
                sel_t = jnp.concatenate([sel_t, jnp.zeros((LANES - nb_pad, tq), F32)], axis=0)
            extra = sel_t.T
        qa_scr[hh, :, 0:LANES] = q
        qa_scr[hh, :, LANES:KW] = extra.astype(BF16)
        m_scr[hh] = jnp.full((SUBLANES, tq), NEG_INF, F32)
        acc_scr[hh] = jnp.zeros((VT_ROWS, tq), F32)

    def fill(s_buf, j):
        k0 = pl.multiple_of(j * tq, tq)
        if not fox:
            kpos = lax.broadcasted_iota(jnp.int32, (tq, LANES), 0) + k0
            lane = lax.broadcasted_iota(jnp.int32, (tq, LANES), 1)
            onehot = jnp.where(lane == lax.shift_right_logical(kpos, int(math.log2(MOBA_BLOCK))), 1.0, 0.0).astype(BF16)
        for hh in range(hps):
            if fox:
                kt = ka_ref[0, hh, pl.ds(k0, tq), :]
            else:
                kt = jnp.concatenate([ka_ref[0, pl.ds(k0, tq), hh * LANES:(hh + 1) * LANES], onehot], axis=1)
            s_buf[hh] = _dot_t(kt, qa_scr[hh])

    def consume(s_buf, j, masked):
        k0 = pl.multiple_of(j * tq, tq)
        for hh in range(hps):
            vt = vt_ref[0, hh, :, pl.ds(k0, tq)]
            s = s_buf[hh]
            if masked:
                key = lax.broadcasted_iota(jnp.int32, (tq, tq), 0)
                qry = lax.broadcasted_iota(jnp.int32, (tq, tq), 1)
                s = jnp.where(key <= qry, s, NEG_INF)
            m_prev = m_scr[hh, 0:1, :]
            m_new = jnp.maximum(m_prev, jnp.max(s, axis=0, keepdims=True))
            alpha = jnp.exp2(m_prev - m_new)
            p = jnp.exp2(s - m_new).astype(BF16)
            acc_scr[hh] = alpha * acc_scr[hh] + _dot(vt, p)
            m_scr[hh] = jnp.broadcast_to(m_new, (SUBLANES, tq))

    fill(sa_scr, 0)
    npairs = i // 2
    nquads = npairs // 2

    def pair_at(j0):
        fill(sb_scr, j0 + 1)
        consume(sa_scr, j0, False)
        fill(sa_scr, j0 + 2)
        consume(sb_scr, j0 + 1, False)

    def quad(t, carry):
        for u in range(4):
            pair_at(8 * t + 2 * u)
        return carry

    nquads = npairs // 4
    lax.fori_loop(0, nquads, quad, 0)

    def pair(t, carry):
        pair_at(2 * t)
        return carry

    lax.fori_loop(4 * nquads, npairs, pair, 0)

    @pl.when(i == 2 * npairs)
    def _():
        consume(sa_scr, i, True)

    @pl.when(i != 2 * npairs)
    def _():
        fill(sb_scr, i)
        consume(sa_scr, i - 1, False)
        consume(sb_scr, i, True)

    for hh in range(hps):
        acc = acc_scr[hh]
        o_t = acc[0:LANES, :] / acc[LANES:LANES + 1, :]
        out = o_t.T
        if fox:
            out = out * _sigmoid(og_ref[0, :, hh * LANES:(hh + 1) * LANES].astype(F32))
        o_ref[0, :, hh * LANES:(hh + 1) * LANES] = out.astype(BF16)


def _attention(arr, q_blk, k_blk, ka, vt, cum_t, bsz, seq, nh, gate_arr=None, gate_blk=0):
    fox = cum_t is not None
    hps = HEADS_PER_STEP
    d = nh * LANES
    tq = _pick(seq, 512, LANES)
    nb = seq // MOBA_BLOCK
    nb_pad = -(-nb // SUBLANES) * SUBLANES
    if fox:
        key_spec = pl.BlockSpec((1, hps, seq, KW), lambda b, h, i: (b, h, 0, 0))
    else:
        key_spec = pl.BlockSpec((1, seq, hps * LANES), lambda b, h, i: (b, 0, k_blk + h))
        ka = arr
    in_specs = [
        pl.BlockSpec((1, tq, hps * LANES), lambda b, h, i: (b, i, q_blk + h)),
        key_spec,
        pl.BlockSpec((1, hps, VT_ROWS, seq), lambda b, h, i: (b, h, 0, 0)),
    ]
    args = [arr, ka, vt]
    scratch = [pltpu.VMEM((hps, tq, KW), BF16), pltpu.VMEM((hps, SUBLANES, tq), F32),
               pltpu.VMEM((hps, VT_ROWS, tq), F32), pltpu.VMEM((hps, tq, tq), F32),
               pltpu.VMEM((hps, tq, tq), F32)]
    if fox:
        in_specs.append(pl.BlockSpec((1, tq, LANES), lambda b, h, i: (b, i, 0)))
        in_specs.append(pl.BlockSpec((1, tq, hps * LANES), lambda b, h, i: (b, i, gate_blk + h)))
        args += [cum_t, gate_arr]
    else:
        assert seq % MOBA_BLOCK == 0 and nb_pad <= LANES and tq % MOBA_BLOCK == 0
        ind = (jnp.arange(LANES)[:, None] == (jnp.arange(seq) // MOBA_BLOCK)[None, :]).astype(BF16)
        in_specs.append(pl.BlockSpec((LANES, seq), lambda b, h, i: (0, 0)))
        args.append(ind)
        scratch.append(pltpu.VMEM((hps, LANES, LANES), BF16))
    out = pl.pallas_call(
        functools.partial(_attn_kernel, fox=fox, tq=tq, nb_pad=nb_pad, topk=MOBA_TOPK),
        out_shape=jax.ShapeDtypeStruct((bsz, seq, d), BF16),
        grid=(bsz, nh // hps, seq // tq),
        in_specs=in_specs,
        out_specs=pl.BlockSpec((1, tq, hps * LANES), lambda b, h, i: (b, i, h)),
        scratch_shapes=scratch,
        compiler_params=_params("parallel", "parallel", "arbitrary"),
        name="fox_attention" if fox else "moba_attention",
    )(*args)
    return out.reshape(bsz * seq, d)


def kernel(x, c, norm1_g, norm2_g, mod_w, mod_b, mlp_w1, mlp_w2, conv_w_in, conv_b_in, conv_dw_w, conv_dw_b,
           conv_ln_g, conv_ln_b, conv_w_out, conv_b_out, moba_w_qkv, moba_q_g, moba_k_g, moba_w_o, fox_w_in,
           fox_b_f, fox_q_g, fox_k_g, fox_w_o):
    bsz, seq, d = x.shape
    depth = mod_w.shape[0]
    hd = moba_q_g.shape[-1]
    nh = d // hd
    assert hd == LANES and nh <= LANES and nh % HEADS_PER_STEP == 0
    hps = HEADS_PER_STEP
    qscale = hd ** -0.5 * LOG2E

    mod = _mod_vectors(c, mod_w, mod_b)
    xf = x.reshape(bsz * seq, d)

    for i in range(depth):
        kind = i % N_MIXERS
        jx = i // N_MIXERS
        modv = mod[i].reshape(bsz * 6, 1, d)
        if kind == 0:
            u = _conv_in(xf, norm1_g[i], modv, conv_w_in[jx].astype(BF16), conv_b_in[jx], seq)
            xf = _conv_out(u, xf, modv, conv_dw_w[jx], conv_dw_b[jx], conv_ln_g[jx], conv_ln_b[jx],
                           conv_w_out[jx].astype(BF16), conv_b_out[jx], seq)
        elif kind == 1:
            w = moba_w_qkv[jx].astype(BF16)
            gvec = jnp.concatenate([jnp.tile(moba_q_g[jx], nh) * qscale, jnp.tile(moba_k_g[jx], nh)]).reshape(1, 2 * d)
            qk, h = _heads_proj(xf, norm1_g[i], modv, w, 2 * d, gvec, seq)
            vt = _vt_proj(h, w, 2 * d, bsz, seq, nh)
            qk = qk.reshape(bsz, seq, 2 * d)
            o = _attention(qk, 0, nh // hps, None, vt, None, bsz, seq, nh)
            xf = _oproj(o, moba_w_o[jx].astype(BF16), xf, modv, seq)
        else:
            w_in = fox_w_in[jx]
            w = w_in[:, :4 * d].astype(BF16)
            gvec = jnp.concatenate([jnp.tile(fox_q_g[jx], nh) * qscale, jnp.tile(fox_k_g[jx], nh)]).reshape(1, 2 * d)
            wf = jnp.pad(w_in[:, 4 * d:], ((0, 0), (0, LANES - nh))).astype(BF16)
            bf = jnp.pad(fox_b_f[jx], (0, LANES - nh)).reshape(1, LANES)
            qk, h, lf = _heads_proj(xf, norm1_g[i], modv, w, 2 * d, gvec, seq, wf, bf)
            vt = _vt_proj(h, w, 2 * d, bsz, seq, nh)
            og = _plain_proj(h, w, 3 * d, d, seq).reshape(bsz, seq, d)
            lf = lf[:, :nh].reshape(bsz, seq, nh).transpose(0, 2, 1)
            cum = _cumsum_seq(lf)
            cum_t = jnp.pad(cum.transpose(0, 2, 1), ((0, 0), (0, 0), (0, LANES - nh)))
            qk = qk.reshape(bsz, seq, 2 * d)
            ka = _fox_keys(qk, nh // hps, cum_t, bsz, seq, nh)
            o = _attention(qk, 0, 0, ka, vt, cum_t, bsz, seq, nh, gate_arr=og, gate_blk=0)
            xf = _oproj(o, fox_w_o[jx].astype(BF16), xf, modv, seq)
        xf = _mlp(xf, norm2_g[i], modv, mlp_w1[i].astype(BF16), mlp_w2[i].astype(BF16), seq)
    return xf.reshape(bsz, seq, d)
```
